```python
import jax, jax.numpy as jnp
from jax import lax
import numpy as np

D_MODEL = 1024
BATCH = 4
SEQ = 4096
DEPTH = 4

CHUNK = 64
N_MIXERS = 2
EPS = 1e-6
GM_BLOCK = 128
GM_HEADS = 8
GM_WIDTH = 2 * D_MODEL
GM_HEAD_DIM = GM_WIDTH // GM_HEADS
HG_EXPAND = 128
HG_HEADS = D_MODEL // HG_EXPAND
HG_KEY = HG_EXPAND
HG_VAL = D_MODEL // HG_HEADS
FFN_HIDDEN = 2816
CONV_WIDTH = 3
N_A = (DEPTH + 1) // 2
N_B = DEPTH // 2

kernel_name = "hybrid_gmlp_hgrn2_convffn_adaln"


def rms_norm(x, g):
    xf = x.astype(jnp.float32)
    y = xf * lax.rsqrt(jnp.mean(xf * xf, axis=-1, keepdims=True) + EPS)
    return (y * g.astype(jnp.float32)).astype(x.dtype)


def chunk_causal_mask(n):
    idx = jnp.arange(n) // CHUNK
    return idx[:, None] >= idx[None, :]


def spatial_gating_mixer(h, w_in, ln_g, ln_b, w_s, b_s, w_out):
    bsz, t, _ = h.shape
    z = jax.nn.gelu(h @ w_in, approximate=False)
    u, v = jnp.split(z, 2, axis=-1)
    vf = v.astype(jnp.float32)
    mu = jnp.mean(vf, axis=-1, keepdims=True)
    var = jnp.mean(jnp.square(vf - mu), axis=-1, keepdims=True)
    v = ((vf - mu) * lax.rsqrt(var + EPS) * ln_g + ln_b).astype(h.dtype)
    v = v.reshape(bsz, t // GM_BLOCK, GM_BLOCK, GM_HEADS, GM_HEAD_DIM)
    ws = jnp.where(chunk_causal_mask(GM_BLOCK)[None], w_s, 0)
    s = jnp.einsum('hnm,bcmhd->bcnhd', ws, v) + b_s.T[None, None, :, :, None]
    gated = u * s.reshape(bsz, t, GM_WIDTH)
    return gated @ w_out


def hgrn2_chunked_scan(q, k, v, logf):
    bsz, t, nh, dk = q.shape
    dv = v.shape[-1]
    nc = t // CHUNK

    def to_chunks(a):
        return a.reshape(bsz, nc, CHUNK, nh, a.shape[-1]).transpose(1, 0, 3, 2, 4)

    causal = jnp.tril(jnp.ones((CHUNK, CHUNK), dtype=bool))[:, :, None]

    def step(state, inp):
        qc, kc, vc, gc = inp
        cum = jnp.cumsum(gc, axis=-2)
        rel = cum[..., :, None, :] - cum[..., None, :, :]
        decay = jnp.exp(jnp.where(causal, rel, -jnp.inf))
        scores = jnp.einsum('bhik,bhjk,bhijk->bhij', qc, kc, decay)
        out = jnp.einsum('bhij,bhjv->bhiv', scores, vc) + \
            jnp.einsum('bhik,bhkv->bhiv', qc * jnp.exp(cum), state)
        last = cum[..., -1:, :]
        state = jnp.exp(last)[..., 0, :, None] * state + \
            jnp.einsum('bhjk,bhjv->bhkv', kc * jnp.exp(last - cum), vc)
        return state, out

    s0 = jnp.zeros((bsz, nh, dk, dv), jnp.float32)
    _, out = lax.scan(step, s0, (to_chunks(q), to_chunks(k), to_chunks(v), to_chunks(logf)))
    return out.transpose(1, 0, 3, 2, 4).reshape(bsz, t, nh, dv)


def hgrn2_mixer(h, w_in, lb, gn_g, w_out):
    bsz, t, _ = h.shape
    q, fz, i, g = jnp.split(h @ w_in, 4, axis=-1)
    f = lb + (1.0 - lb) * jax.nn.sigmoid(fz.astype(jnp.float32))
    logf = jnp.log(f)
    k = 1.0 - f
    q = jax.nn.silu(q).astype(jnp.float32)

    def heads(a):
        return a.reshape(bsz, t, HG_HEADS, -1)

    o = hgrn2_chunked_scan(heads(q), heads(k), heads(i.astype(jnp.float32)), heads(logf))
    o = o * lax.rsqrt(jnp.mean(o * o, axis=-1, keepdims=True) + EPS)
    o = (o.reshape(bsz, t, HG_HEADS * HG_VAL) * gn_g.astype(jnp.float32)).astype(h.dtype)
    return (o * jax.nn.silu(g)) @ w_out


def conv_ffn(h, w_up, conv_w, conv_b, w_down):
    t = h.shape[1]
    a = h @ w_up
    ap = jnp.pad(a, ((0, 0), (CONV_WIDTH - 1, 0), (0, 0)))
    y = conv_b
    for j in range(CONV_WIDTH):
        y = y + conv_w[j] * ap[:, j:j + t]
    gate, val = jnp.split(y, 2, axis=-1)
    return (jax.nn.gelu(gate, approximate=False) * val) @ w_down


def setup_inputs(seed: int = 0) -> dict:
    key = jax.random.key(seed)
    ks = jax.random.split(key, 20)
    nrm = jax.random.normal
    f32 = jnp.float32
    D, F2 = D_MODEL, 2 * FFN_HIDDEN
    return {
        "x": nrm(ks[0], (BATCH, SEQ, D), f32),
        "c": nrm(ks[1], (BATCH, D), f32),
        "gm_w_in": nrm(ks[2], (N_A, D, 2 * GM_WIDTH), f32) * D ** -0.5,
        "gm_ln_g": 1.0 + 0.02 * nrm(ks[3], (N_A, GM_WIDTH), f32),
        "gm_ln_b": 0.02 * nrm(ks[4], (N_A, GM_WIDTH), f32),
        "gm_w_s": nrm(ks[5], (N_A, GM_HEADS, GM_BLOCK, GM_BLOCK), f32) * GM_BLOCK ** -0.5,
        "gm_b_s": 1.0 + 0.1 * nrm(ks[6], (N_A, GM_HEADS, GM_BLOCK), f32),
        "gm_w_out": nrm(ks[7], (N_A, GM_WIDTH, D), f32) * GM_WIDTH ** -0.5,
        "hg_w_in": nrm(ks[8], (N_B, D, 4 * D), f32) * D ** -0.5,
        "hg_lb": 0.5 * nrm(ks[9], (N_B, D), f32),
        "hg_gn_g": 1.0 + 0.02 * nrm(ks[10], (N_B, D), f32),
        "hg_w_out": nrm(ks[11], (N_B, D, D), f32) * D ** -0.5,
        "ffn_w_up": nrm(ks[12], (DEPTH, D, F2), f32) * D ** -0.5,
        "ffn_conv_w": nrm(ks[13], (DEPTH, CONV_WIDTH, F2), f32) * CONV_WIDTH ** -0.5,
        "ffn_conv_b": 0.02 * nrm(ks[14], (DEPTH, F2), f32),
        "ffn_w_down": nrm(ks[15], (DEPTH, FFN_HIDDEN, D), f32) * FFN_HIDDEN ** -0.5,
        "norm_g": 1.0 + 0.02 * nrm(ks[16], (DEPTH, 2, D), f32),
        "ada_w": nrm(ks[17], (DEPTH, D, 6 * D), f32) * D ** -0.5,
        "ada_b": 0.02 * nrm(ks[18], (DEPTH, 6 * D), f32),
        "final_g": 1.0 + 0.02 * nrm(ks[19], (D,), f32),
    }


def reference(x, c, gm_w_in, gm_ln_g, gm_ln_b, gm_w_s, gm_b_s, gm_w_out,
              hg_w_in, hg_lb, hg_gn_g, hg_w_out,
              ffn_w_up, ffn_conv_w, ffn_conv_b, ffn_w_down,
              norm_g, ada_w, ada_b, final_g):
    lb_p = jax.nn.softmax(hg_lb.astype(jnp.float32), axis=0)
    lb_all = jnp.cumsum(lb_p, axis=0) - lb_p[0]
    cond = jax.nn.silu(c)
    for i in range(DEPTH):
        mod = (cond @ ada_w[i] + ada_b[i])[:, None, :]
        sh1, sc1, g1, sh2, sc2, g2 = jnp.split(mod, 6, axis=-1)
        h = rms_norm(x, norm_g[i, 0]) * (1.0 + sc1) + sh1
        j = i // N_MIXERS
        if i % N_MIXERS == 0:
            y = spatial_gating_mixer(h, gm_w_in[j], gm_ln_g[j], gm_ln_b[j],
                                     gm_w_s[j], gm_b_s[j], gm_w_out[j])
        else:
            y = hgrn2_mixer(h, hg_w_in[j], lb_all[j], hg_gn_g[j], hg_w_out[j])
        x = x + g1 * y
        h = rms_norm(x, norm_g[i, 1]) * (1.0 + sc2) + sh2
        x = x + g2 * conv_ffn(h, ffn_w_up[i], ffn_conv_w[i], ffn_conv_b[i], ffn_w_down[i])
    return rms_norm(x, final_g)
```

```python
import functools

import jax
import jax.numpy as jnp
from jax import lax
from jax.experimental import pallas as pl
from jax.experimental.pallas import tpu as pltpu

F32 = jnp.float32
BF16 = jnp.bfloat16

EPS = 1e-6
CHUNK = 64
GM_BLOCK = 128
GM_HEADS = 8
HG_HEADS = 8
HG_DIM = 128
SCAN_CHUNK = 64
DIAG = 8
CONV_WIDTH = 3
SQRT_HALF = 0.7071067811865476

TOKEN_TILE = 512
VMEM_LIMIT_BYTES = 56 * 1024 * 1024


def _gelu(x):
    return 0.5 * x * (1.0 + lax.erf(x * SQRT_HALF))


def _silu(x):
    return x * jax.nn.sigmoid(x)


def _modulated_rms_norm(x, gain, scale, shift):
    ms = jnp.mean(x * x, axis=-1, keepdims=True)
    return (x * lax.rsqrt(ms + EPS)) * (gain * (1.0 + scale)) + shift


def _resident(shape):
    zeros = (0,) * len(shape)
    return pl.BlockSpec(shape, lambda *_: zeros, pipeline_mode=pl.Buffered(1))


def _ada_kernel(c_ref, w_ref, b_ref, o_ref):
    cond = _silu(c_ref[...])
    o_ref[0] = jnp.dot(cond, w_ref[0], preferred_element_type=F32) + b_ref[0]


def _ada_modulation(c_pad, ada_w, ada_b):
    depth, d, n = ada_w.shape
    rows = c_pad.shape[0]
    tn = 1536
    return pl.pallas_call(
        _ada_kernel,
        grid=(depth, n // tn),
        in_specs=[
            pl.BlockSpec((rows, d), lambda l, j: (0, 0)),
            pl.BlockSpec((1, d, tn), lambda l, j: (l, 0, j)),
            pl.BlockSpec((1, 1, tn), lambda l, j: (l, 0, j)),
        ],
        out_specs=pl.BlockSpec((1, rows, tn), lambda l, j: (l, 0, j)),
        out_shape=jax.ShapeDtypeStruct((depth, rows, n), F32),
        compiler_params=pltpu.CompilerParams(
            dimension_semantics=("arbitrary", "arbitrary"),
            vmem_limit_bytes=VMEM_LIMIT_BYTES),
        name="ada_modulation",
    )(c_pad, ada_w, ada_b.reshape(depth, 1, n))


def _gmlp_kernel(x_ref, mod_ref, ng_ref, win_ref, lng_ref, lnb_ref, ws_ref,
                 bs_ref, wout_ref, o_ref, h_scr, u_scr, v_scr, vn_scr, gt_scr):
    tm, d = x_ref.shape
    width = u_scr.shape[1]
    head_dim = width // GM_HEADS
    x = x_ref[...]
    mod = mod_ref[0]
    h = _modulated_rms_norm(x, ng_ref[...], mod[1:2], mod[0:1])
    h_scr[...] = h.astype(BF16)

    cn = 512
    for j in range(2 * width // cn):
        z = jnp.dot(h_scr[...], win_ref[:, j * cn:(j + 1) * cn],
                    preferred_element_type=F32)
        z = _gelu(z)
        if j * cn < width:
            u_scr[:, j * cn:(j + 1) * cn] = z
        else:
            v_scr[:, j * cn - width:(j + 1) * cn - width] = z

    v = v_scr[...]
    mu = jnp.mean(v, axis=-1, keepdims=True)
    dv = v - mu
    var = jnp.mean(dv * dv, axis=-1, keepdims=True)
    vn = dv * lax.rsqrt(var + EPS) * lng_ref[...] + lnb_ref[...]
    vn_scr[...] = vn.astype(BF16)

    row = lax.broadcasted_iota(jnp.int32, (GM_BLOCK, GM_BLOCK), 0)
    col = lax.broadcasted_iota(jnp.int32, (GM_BLOCK, GM_BLOCK), 1)
    allowed = (row >= CHUNK) | (col < CHUNK)
    for hd in range(GM_HEADS):
        w = jnp.where(allowed, ws_ref[hd], 0.0).astype(BF16)
        cols = slice(hd * head_dim, (hd + 1) * head_dim)
        for r in range(tm // GM_BLOCK):
            rows = slice(r * GM_BLOCK, (r + 1) * GM_BLOCK)
            s = jnp.dot(w, vn_scr[rows, cols], preferred_element_type=F32)
            s = s + bs_ref[:, cols]
            gt_scr[rows, cols] = (u_scr[rows, cols] * s).astype(BF16)

    y = jnp.dot(gt_scr[...], wout_ref[...], preferred_element_type=F32)
    o_ref[...] = x + mod[2:3] * y


def _gmlp_layer(x2, mod, norm_g, w_in, ln_g, ln_b, w_s, b_s_tile, w_out, seq):
    n, d = x2.shape
    tm = TOKEN_TILE
    width = w_out.shape[0]
    tiles_per_seq = seq // tm
    return pl.pallas_call(
        _gmlp_kernel,
        grid=(n // tm,),
        in_specs=[
            pl.BlockSpec((tm, d), lambda t: (t, 0)),
            pl.BlockSpec((1, 6, d), lambda t: (t // tiles_per_seq, 0, 0)),
            _resident((1, d)),
            _resident(w_in.shape),
            _resident((1, width)),
            _resident((1, width)),
            _resident(w_s.shape),
            _resident(b_s_tile.shape),
            _resident(w_out.shape),
        ],
        out_specs=pl.BlockSpec((tm, d), lambda t: (t, 0)),
        out_shape=jax.ShapeDtypeStruct((n, d), F32),
        scratch_shapes=[
            pltpu.VMEM((tm, d), BF16),
            pltpu.VMEM((tm, width), F32),
            pltpu.VMEM((tm, width), F32),
            pltpu.VMEM((tm, width), BF16),
            pltpu.VMEM((tm, width), BF16),
        ],
        compiler_params=pltpu.CompilerParams(
            dimension_semantics=("arbitrary",),
            vmem_limit_bytes=VMEM_LIMIT_BYTES),
        name="gmlp_mixer",
    )(x2, mod, norm_g.reshape(1, d), w_in, ln_g.reshape(1, width),
      ln_b.reshape(1, width), w_s, b_s_tile, w_out)


def _chunk_cumsum(g, row):
    cum = g
    step = 1
    while step < SCAN_CHUNK:
        cum = cum + jnp.where(row >= step, pltpu.roll(cum, step, 0), 0.0)
        step *= 2
    return cum


def _block_boundary(cum, half):
    pieces = []
    for start in range(0, SCAN_CHUNK, 2 * half):
        last_upper = start + half - 1
        pieces.append(jnp.broadcast_to(cum[last_upper:last_upper + 1, :],
                                       (2 * half, cum.shape[1])))
    return pieces[0] if len(pieces) == 1 else jnp.concatenate(pieces, axis=0)


def _hgrn_chunk_head(q, k, v, cum, state_t, row, row_s, col_s):
    last = cum[SCAN_CHUNK - 1:SCAN_CHUNK, :]
    v16 = v.astype(BF16)

    q_in = (q * jnp.exp(cum)).astype(BF16)
    out = lax.dot_general(q_in, state_t.astype(BF16), (((1,), (1,)), ((), ())),
                          preferred_element_type=F32)

    scores = None
    half = SCAN_CHUNK // 2
    while half >= DIAG:
        lower = (row & (2 * half - 1)) >= half
        bnd = _block_boundary(cum, half)
        e = jnp.exp(jnp.where(lower, cum - bnd, bnd - cum))
        q_l = jnp.where(lower, q * e, 0.0).astype(BF16)
        k_u = jnp.where(lower, 0.0, k * e).astype(BF16)
        s = lax.dot_general(q_l, k_u, (((1,), (1,)), ((), ())),
                            preferred_element_type=F32)
        if 2 * half < SCAN_CHUNK:
            same = ((row_s ^ col_s) & -(2 * half)) == 0
            s = jnp.where(same, s, 0.0)
        scores = s if scores is None else scores + s
        half //= 2
    out = out + jnp.dot(scores.astype(BF16), v16, preferred_element_type=F32)

    out = out + jnp.sum(q * k, axis=-1, keepdims=True) * v
    for off in range(1, DIAG):
        valid = (row & (DIAG - 1)) >= off
        k_s = pltpu.roll(k, off, 0)
        c_s = pltpu.roll(cum, off, 0)
        v_s = pltpu.roll(v, off, 0)
        t = q * k_s * jnp.exp(jnp.where(valid, cum - c_s, 0.0))
        a = jnp.sum(jnp.where(valid, t, 0.0), axis=-1, keepdims=True)
        out = out + a * v_s

    k_out = (k * jnp.exp(last - cum)).astype(BF16)
    new_state_t = state_t * jnp.exp(last) + lax.dot_general(
        v16, k_out, (((0,), (0,)), ((), ())), preferred_element_type=F32)
    return out, new_state_t


def _hgrn_lower_bound(lb_raw, layer):
    e = jnp.exp(lb_raw - jnp.max(lb_raw, axis=0, keepdims=True))
    total = jnp.sum(e, axis=0, keepdims=True)
    below = jnp.zeros_like(total)
    for m in range(1, layer + 1):
        below = below + e[m:m + 1]
    return below / total


def _hgrn_kernel(x_ref, mod_ref, ng_ref, win_ref, lb_ref, gn_ref, wout_ref,
                 o_ref, h_scr, q_scr, k_scr, v_scr, lf_scr, sg_scr, o_scr,
                 state_scr, *, layer):
    tm, d = x_ref.shape

    @pl.when(pl.program_id(1) == 0)
    def _():
        state_scr[...] = jnp.zeros_like(state_scr)

    x = x_ref[...]
    mod = mod_ref[0]
    h = _modulated_rms_norm(x, ng_ref[...], mod[1:2], mod[0:1])
    h_scr[...] = h.astype(BF16)

    def proj(i):
        return jnp.dot(h_scr[...], win_ref[:, i * d:(i + 1) * d],
                       preferred_element_type=F32)

    q_scr[...] = _silu(proj(0))
    lb = _hgrn_lower_bound(lb_ref[...], layer)
    f = lb + (1.0 - lb) * jax.nn.sigmoid(proj(1))
    lf_scr[...] = jnp.log(f)
    k_scr[...] = 1.0 - f
    v_scr[...] = proj(2)
    sg_scr[...] = _silu(proj(3))

    row = lax.broadcasted_iota(jnp.int32, (SCAN_CHUNK, HG_DIM), 0)
    row_s = lax.broadcasted_iota(jnp.int32, (SCAN_CHUNK, SCAN_CHUNK), 0)
    col_s = lax.broadcasted_iota(jnp.int32, (SCAN_CHUNK, SCAN_CHUNK), 1)

    def chunk_body(c, carry):
        rows = pl.ds(pl.multiple_of(c * SCAN_CHUNK, SCAN_CHUNK), SCAN_CHUNK)
        for hd in range(HG_HEADS):
            lanes = slice(hd * HG_DIM, (hd + 1) * HG_DIM)
            cum = _chunk_cumsum(lf_scr[rows, lanes], row)
            out, new_state = _hgrn_chunk_head(
                q_scr[rows, lanes], k_scr[rows, lanes], v_scr[rows, lanes],
                cum, state_scr[hd], row, row_s, col_s)
            state_scr[hd] = new_state
            ms = jnp.mean(out * out, axis=-1, keepdims=True)
            o_scr[rows, lanes] = out * lax.rsqrt(ms + EPS)
        return carry

    lax.fori_loop(0, tm // SCAN_CHUNK, chunk_body, 0)

    gated = (o_scr[...] * gn_ref[...]) * sg_scr[...]
    y = jnp.dot(gated.astype(BF16), wout_ref[...], preferred_element_type=F32)
    o_ref[...] = x + mod[2:3] * y


def _hgrn_layer(x2, mod, norm_g, w_in, lb_raw, layer, gn_g, w_out, batch, seq):
    n, d = x2.shape
    tm = TOKEN_TILE
    tiles_per_seq = seq // tm
    act = pltpu.VMEM((tm, d), F32)
    return pl.pallas_call(
        functools.partial(_hgrn_kernel, layer=layer),
        grid=(batch, tiles_per_seq),
        in_specs=[
            pl.BlockSpec((tm, d), lambda b, t: (b * tiles_per_seq + t, 0)),
            pl.BlockSpec((1, 6, d), lambda b, t: (b, 0, 0)),
            _resident((1, d)),
            _resident(w_in.shape),
            _resident(lb_raw.shape),
            _resident((1, d)),
            _resident(w_out.shape),
        ],
        out_specs=pl.BlockSpec((tm, d), lambda b, t: (b * tiles_per_seq + t, 0)),
        out_shape=jax.ShapeDtypeStruct((n, d), F32),
        scratch_shapes=[
            pltpu.VMEM((tm, d), BF16),
            act, act, act, act, act, act,
            pltpu.VMEM((HG_HEADS, HG_DIM, HG_DIM), F32),
        ],
        compiler_params=pltpu.CompilerParams(
            dimension_semantics=("arbitrary", "arbitrary"),
            vmem_limit_bytes=VMEM_LIMIT_BYTES),
        name="hgrn_mixer",
    )(x2, mod, norm_g.reshape(1, d), w_in, lb_raw.astype(F32),
      gn_g.reshape(1, d), w_out)


def _ffn_kernel(x_ref, mod_ref, ng_ref, wup_ref, cw_ref, cb_ref, wdown_ref,
                fg_ref, o_ref, h_scr, p_scr, halo_scr, *, final_norm):
    tm, d = x_ref.shape
    hidden = wdown_ref.shape[0]

    @pl.when(pl.program_id(1) == 0)
    def _():
        halo_scr[...] = jnp.zeros_like(halo_scr)

    x = x_ref[...]
    mod = mod_ref[0]
    h = _modulated_rms_norm(x, ng_ref[...], mod[4:5], mod[3:4])
    h_scr[...] = h.astype(BF16)

    cn = 256
    row = lax.broadcasted_iota(jnp.int32, (tm, cn), 0)

    def conv(cols):
        a = jnp.dot(h_scr[...], wup_ref[:, cols], preferred_element_type=F32)
        prev1 = halo_scr[1:2, cols]
        prev2 = halo_scr[0:1, cols]
        halo_scr[:, cols] = a[tm - 2:tm, :]
        a1 = jnp.where(row == 0, prev1, pltpu.roll(a, 1, 0))
        a2 = jnp.where(row == 0, prev2,
                       jnp.where(row == 1, prev1, pltpu.roll(a, 2, 0)))
        return (cb_ref[:, cols] + cw_ref[0:1, cols] * a2
                + cw_ref[1:2, cols] * a1 + cw_ref[2:3, cols] * a)

    for j in range(hidden // cn):
        gate = conv(slice(j * cn, (j + 1) * cn))
        val = conv(slice(hidden + j * cn, hidden + (j + 1) * cn))
        p_scr[:, j * cn:(j + 1) * cn] = (_gelu(gate) * val).astype(BF16)

    y = jnp.dot(p_scr[...], wdown_ref[...], preferred_element_type=F32)
    out = x + mod[5:6] * y
    if final_norm:
        ms = jnp.mean(out * out, axis=-1, keepdims=True)
        out = (out * lax.rsqrt(ms + EPS)) * fg_ref[...]
    o_ref[...] = out


def _ffn_layer(x2, mod, norm_g, w_up, conv_w, conv_b, w_down, final_g, batch,
               seq, final_norm):
    n, d = x2.shape
    tm = TOKEN_TILE
    tiles_per_seq = seq // tm
    hidden = w_down.shape[0]
    return pl.pallas_call(
        functools.partial(_ffn_kernel, final_norm=final_norm),
        grid=(batch, tiles_per_seq),
        in_specs=[
            pl.BlockSpec((tm, d), lambda b, t: (b * tiles_per_seq + t, 0)),
            pl.BlockSpec((1, 6, d), lambda b, t: (b, 0, 0)),
            _resident((1, d)),
            _resident(w_up.shape),
            _resident(conv_w.shape),
            _resident((1, 2 * hidden)),
            _resident(w_down.shape),
            _resident((1, d)),
        ],
        out_specs=pl.BlockSpec((tm, d), lambda b, t: (b * tiles_per_seq + t, 0)),
        out_shape=jax.ShapeDtypeStruct((n, d), F32),
        scratch_shapes=[
            pltpu.VMEM((tm, d), BF16),
            pltpu.VMEM((tm, hidden), BF16),
            pltpu.VMEM((CONV_WIDTH - 1, 2 * hidden), F32),
        ],
        compiler_params=pltpu.CompilerParams(
            dimension_semantics=("arbitrary", "arbitrary"),
            vmem_limit_bytes=VMEM_LIMIT_BYTES),
        name="conv_ffn",
    )(x2, mod, norm_g.reshape(1, d), w_up, conv_w, conv_b.reshape(1, 2 * hidden),
      w_down, final_g.reshape(1, d))


def kernel(x, c, gm_w_in, gm_ln_g, gm_ln_b, gm_w_s, gm_b_s, gm_w_out, hg_w_in, hg_lb, hg_gn_g, hg_w_out, ffn_w_up, ffn_conv_w, ffn_conv_b, ffn_w_down, norm_g, ada_w, ada_b, final_g):
    batch, seq, d = x.shape
    depth = ada_w.shape[0]
    assert seq % TOKEN_TILE == 0 and TOKEN_TILE % GM_BLOCK == 0

    c_pad = jnp.pad(c, ((0, 8 - batch), (0, 0)))
    mod_all = _ada_modulation(c_pad, ada_w, ada_b)
    mod_all = mod_all[:, :batch].reshape(depth, batch, 6, d)

    head_dim = gm_w_out.shape[1] // GM_HEADS
    x2 = x.reshape(batch * seq, d)
    for i in range(depth):
        j = i // 2
        if i % 2 == 0:
            b_s_tile = jnp.repeat(gm_b_s[j].T, head_dim, axis=1)
            x2 = _gmlp_layer(x2, mod_all[i], norm_g[i, 0],
                             gm_w_in[j].astype(BF16), gm_ln_g[j], gm_ln_b[j],
                             gm_w_s[j], b_s_tile, gm_w_out[j].astype(BF16), seq)
        else:
            x2 = _hgrn_layer(x2, mod_all[i], norm_g[i, 0],
                             hg_w_in[j].astype(BF16), hg_lb, j, hg_gn_g[j],
                             hg_w_out[j].astype(BF16), batch, seq)
        x2 = _ffn_layer(x2, mod_all[i], norm_g[i, 1], ffn_w_up[i].astype(BF16),
                        ffn_conv_w[i], ffn_conv_b[i],
                        ffn_w_down[i].astype(BF16), final_g, batch, seq,
                        final_norm=(i == depth - 1))
    return x2.reshape(batch, seq, d)
```

```python
import functools

import jax
import jax.numpy as jnp
from jax import lax
from jax.experimental import pallas as pl
from jax.experimental.pallas import tpu as pltpu

F32 = jnp.float32
BF16 = jnp.bfloat16

EPS = 1e-6
CHUNK = 64
GM_BLOCK = 128
GM_HEADS = 8
HG_HEADS = 8
HG_DIM = 128
SCAN_CHUNK = 64
SCAN_HALVES = (32, 16, 8, 4, 2, 1)
VREG_ROWS = 8
LOG2_E = 1.4426950408889634
CONV_WIDTH = 3
SQRT_HALF = 0.7071067811865476

TOKEN_TILE = 512
VMEM_LIMIT_BYTES = 56 * 1024 * 1024


def _gelu(x):
    return 0.5 * x * (1.0 + lax.erf(x * SQRT_HALF))


def _silu(x):
    return x * jax.nn.sigmoid(x)


def _modulated_rms_norm(x, gain, scale, shift):
    ms = jnp.mean(x * x, axis=-1, keepdims=True)
    return (x * lax.rsqrt(ms + EPS)) * (gain * (1.0 + scale)) + shift


def _resident(shape):
    zeros = (0,) * len(shape)
    return pl.BlockSpec(shape, lambda *_: zeros, pipeline_mode=pl.Buffered(1))


def _ada_kernel(c_ref, w_ref, b_ref, o_ref):
    cond = _silu(c_ref[...])
    o_ref[0] = jnp.dot(cond, w_ref[0], preferred_element_type=F32) + b_ref[0]


def _ada_modulation(c_pad, ada_w, ada_b):
    depth, d, n = ada_w.shape
    rows = c_pad.shape[0]
    tn = 1536
    return pl.pallas_call(
        _ada_kernel,
        grid=(depth, n // tn),
        in_specs=[
            pl.BlockSpec((rows, d), lambda l, j: (0, 0)),
            pl.BlockSpec((1, d, tn), lambda l, j: (l, 0, j)),
            pl.BlockSpec((1, 1, tn), lambda l, j: (l, 0, j)),
        ],
        out_specs=pl.BlockSpec((1, rows, tn), lambda l, j: (l, 0, j)),
        out_shape=jax.ShapeDtypeStruct((depth, rows, n), F32),
        compiler_params=pltpu.CompilerParams(
            dimension_semantics=("arbitrary", "arbitrary"),
            vmem_limit_bytes=VMEM_LIMIT_BYTES),
        name="ada_modulation",
    )(c_pad, ada_w, ada_b.reshape(depth, 1, n))


def _gmlp_kernel(x_ref, mod_ref, ng_ref, win_ref, lng_ref, lnb_ref, ws_ref,
                 bs_ref, wout_ref, o_ref, h_scr, u_scr, v_scr, vn_scr, gt_scr):
    tm, d = x_ref.shape
    width = u_scr.shape[1]
    head_dim = width // GM_HEADS
    x = x_ref[...]
    mod = mod_ref[0]
    h = _modulated_rms_norm(x, ng_ref[...], mod[1:2], mod[0:1])
    h_scr[...] = h.astype(BF16)

    cn = 512
    for j in range(2 * width // cn):
        z = jnp.dot(h_scr[...], win_ref[:, j * cn:(j + 1) * cn],
                    preferred_element_type=F32)
        z = _gelu(z)
        if j * cn < width:
            u_scr[:, j * cn:(j + 1) * cn] = z
        else:
            v_scr[:, j * cn - width:(j + 1) * cn - width] = z

    v = v_scr[...]
    mu = jnp.mean(v, axis=-1, keepdims=True)
    dv = v - mu
    var = jnp.mean(dv * dv, axis=-1, keepdims=True)
    vn = dv * lax.rsqrt(var + EPS) * lng_ref[...] + lnb_ref[...]
    vn_scr[...] = vn.astype(BF16)

    row = lax.broadcasted_iota(jnp.int32, (GM_BLOCK, GM_BLOCK), 0)
    col = lax.broadcasted_iota(jnp.int32, (GM_BLOCK, GM_BLOCK), 1)
    allowed = (row >= CHUNK) | (col < CHUNK)
    for hd in range(GM_HEADS):
        w = jnp.where(allowed, ws_ref[hd], 0.0).astype(BF16)
        cols = slice(hd * head_dim, (hd + 1) * head_dim)
        for r in range(tm // GM_BLOCK):
            rows = slice(r * GM_BLOCK, (r + 1) * GM_BLOCK)
            s = jnp.dot(w, vn_scr[rows, cols], preferred_element_type=F32)
            s = s + bs_ref[:, cols]
            gt_scr[rows, cols] = (u_scr[rows, cols] * s).astype(BF16)

    y = jnp.dot(gt_scr[...], wout_ref[...], preferred_element_type=F32)
    o_ref[...] = x + mod[2:3] * y


def _gmlp_layer(x2, mod, norm_g, w_in, ln_g, ln_b, w_s, b_s_tile, w_out, seq):
    n, d = x2.shape
    tm = TOKEN_TILE
    width = w_out.shape[0]
    tiles_per_seq = seq // tm
    return pl.pallas_call(
        _gmlp_kernel,
        grid=(n // tm,),
        in_specs=[
            pl.BlockSpec((tm, d), lambda t: (t, 0)),
            pl.BlockSpec((1, 6, d), lambda t: (t // tiles_per_seq, 0, 0)),
            _resident((1, d)),
            _resident(w_in.shape),
            _resident((1, width)),
            _resident((1, width)),
            _resident(w_s.shape),
            _resident(b_s_tile.shape),
            _resident(w_out.shape),
        ],
        out_specs=pl.BlockSpec((tm, d), lambda t: (t, 0)),
        out_shape=jax.ShapeDtypeStruct((n, d), F32),
        scratch_shapes=[
            pltpu.VMEM((tm, d), BF16),
            pltpu.VMEM((tm, width), F32),
            pltpu.VMEM((tm, width), F32),
            pltpu.VMEM((tm, width), BF16),
            pltpu.VMEM((tm, width), BF16),
        ],
        compiler_params=pltpu.CompilerParams(
            dimension_semantics=("arbitrary",),
            vmem_limit_bytes=VMEM_LIMIT_BYTES),
        name="gmlp_mixer",
    )(x2, mod, norm_g.reshape(1, d), w_in, ln_g.reshape(1, width),
      ln_b.reshape(1, width), w_s, b_s_tile, w_out)


def _chunk_cumsum(g):
    groups = SCAN_CHUNK // VREG_ROWS
    g3 = g.reshape(groups, VREG_ROWS, g.shape[1])
    sub = lax.broadcasted_iota(jnp.int32, g3.shape, 1)
    step = 1
    while step < VREG_ROWS:
        g3 = g3 + jnp.where(sub >= step, pltpu.roll(g3, step, 1), 0.0)
        step *= 2
    pieces = [g3[0]]
    for r in range(1, groups):
        pieces.append(g3[r] + pieces[-1][VREG_ROWS - 1:VREG_ROWS, :])
    return jnp.concatenate(pieces, axis=0)


def _split_operands(q, k, cum, g, half):
    lanes = q.shape[1]
    if half >= VREG_ROWS:
        zeros = jnp.zeros((half, lanes), F32)
        q_parts, k_parts = [], []
        for start in range(0, SCAN_CHUNK, 2 * half):
            mid = start + half
            bnd = cum[mid - 1:mid, :]
            k_parts += [k[start:mid] * jnp.exp2(bnd - cum[start:mid]), zeros]
            q_parts += [zeros,
                        q[mid:mid + half] * jnp.exp2(cum[mid:mid + half] - bnd)]
        return jnp.concatenate(q_parts, axis=0), jnp.concatenate(k_parts, axis=0)
    if half == 1:
        return q * jnp.exp2(g), k
    groups = SCAN_CHUNK // VREG_ROWS
    cum3 = cum.reshape(groups, VREG_ROWS, lanes)
    sub = lax.broadcasted_iota(jnp.int32, cum3.shape, 1)
    bnd = None
    for start in range(0, VREG_ROWS, 2 * half):
        piece = jnp.broadcast_to(
            cum3[:, start + half - 1:start + half, :], cum3.shape)
        bnd = piece if bnd is None else jnp.where(sub >= start, piece, bnd)
    e = jnp.exp2(-jnp.abs(cum3 - bnd)).reshape(SCAN_CHUNK, lanes)
    return q * e, k * e


def _stage_head(rows, lanes, q_ref, k_ref, v_ref, g_ref, chunk, stage):
    ql_ref, ku_ref, qin_ref, kout_ref, v16_ref, direct_ref, decay_ref = stage
    q, k, v, g = (ref[rows, lanes] for ref in (q_ref, k_ref, v_ref, g_ref))
    cum = _chunk_cumsum(g)
    last = cum[SCAN_CHUNK - 1:SCAN_CHUNK, :]
    qin_ref[rows, lanes] = (q * jnp.exp2(cum)).astype(BF16)
    kout_ref[rows, lanes] = (k * jnp.exp2(last - cum)).astype(BF16)
    v16_ref[rows, lanes] = v.astype(BF16)
    decay_ref[chunk, :, lanes] = jnp.exp2(last)
    direct_ref[rows, lanes] = jnp.sum(q * k, axis=-1, keepdims=True) * v
    for idx, half in enumerate(SCAN_HALVES):
        q_l, k_u = _split_operands(q, k, cum, g, half)
        ql_ref[idx, rows, lanes] = q_l.astype(BF16)
        ku_ref[idx, rows, lanes] = k_u.astype(BF16)


def _block_diag(a, b):
    zeros = jnp.zeros_like(a)
    return jnp.concatenate([jnp.concatenate([a, zeros], axis=1),
                            jnp.concatenate([zeros, b], axis=1)], axis=0)


def _combine_head_pair(pair, rows, chunk, level_of_pair, stage, state_ref):
    ql_ref, ku_ref, qin_ref, kout_ref, v16_ref, direct_ref, decay_ref = stage
    nt = (((1,), (1,)), ((), ()))
    first, second = 2 * pair, 2 * pair + 1
    lanes_a = slice(first * HG_DIM, second * HG_DIM)
    lanes_b = slice(second * HG_DIM, (second + 1) * HG_DIM)
    lanes = slice(first * HG_DIM, (second + 1) * HG_DIM)

    state_a, state_b = state_ref[first], state_ref[second]
    out = direct_ref[rows, lanes] + lax.dot_general(
        qin_ref[rows, lanes],
        _block_diag(state_a.astype(BF16), state_b.astype(BF16)), nt,
        preferred_element_type=F32)
    scores = None
    for idx, half in enumerate(SCAN_HALVES):
        s = lax.dot_general(
            ql_ref[idx, rows, lanes],
            _block_diag(ku_ref[idx, rows, lanes_a], ku_ref[idx, rows, lanes_b]),
            nt, preferred_element_type=F32)
        scores = s if scores is None else jnp.where(level_of_pair[half], s, scores)
    v_bd = _block_diag(v16_ref[rows, lanes_a], v16_ref[rows, lanes_b])
    out = out + jnp.dot(scores.astype(BF16), v_bd, preferred_element_type=F32)
    update = lax.dot_general(
        v_bd, _block_diag(kout_ref[rows, lanes_a], kout_ref[rows, lanes_b]),
        (((0,), (0,)), ((), ())), preferred_element_type=F32)
    state_ref[first] = (state_a * decay_ref[chunk, :, lanes_a]
                        + update[:HG_DIM, :HG_DIM])
    state_ref[second] = (state_b * decay_ref[chunk, :, lanes_b]
                         + update[HG_DIM:, HG_DIM:])
    return out


def _hgrn_lower_bound(lb_raw, layer):
    e = jnp.exp(lb_raw - jnp.max(lb_raw, axis=0, keepdims=True))
    total = jnp.sum(e, axis=0, keepdims=True)
    below = jnp.zeros_like(total)
    for m in range(1, layer + 1):
        below = below + e[m:m + 1]
    return below / total


def _hgrn_kernel(x_ref, mod_ref, ng_ref, win_ref, lb_ref, gn_ref, wout_ref,
                 o_ref, h_scr, q_scr, k_scr, v_scr, lf_scr, sg_scr, o_scr,
                 state_scr, ql_scr, ku_scr, qin_scr, kout_scr, v16_scr,
                 decay_scr, *, layer):
    tm, d = x_ref.shape
    stage = (ql_scr, ku_scr, qin_scr, kout_scr, v16_scr, o_scr, decay_scr)

    @pl.when(pl.program_id(1) == 0)
    def _():
        state_scr[...] = jnp.zeros_like(state_scr)

    x = x_ref[...]
    mod = mod_ref[0]
    h = _modulated_rms_norm(x, ng_ref[...], mod[1:2], mod[0:1])
    h_scr[...] = h.astype(BF16)

    def proj(i):
        return jnp.dot(h_scr[...], win_ref[:, i * d:(i + 1) * d],
                       preferred_element_type=F32)

    q_scr[...] = _silu(proj(0))
    lb = _hgrn_lower_bound(lb_ref[...], layer)
    f = lb + (1.0 - lb) * jax.nn.sigmoid(proj(1))
    lf_scr[...] = jnp.log(f) * LOG2_E
    k_scr[...] = 1.0 - f
    v_scr[...] = proj(2)
    sg_scr[...] = _silu(proj(3))

    head_lanes = [slice(hd * HG_DIM, (hd + 1) * HG_DIM)
                  for hd in range(HG_HEADS)]

    def chunk_rows(c):
        return pl.ds(pl.multiple_of(c * SCAN_CHUNK, SCAN_CHUNK), SCAN_CHUNK)

    def stage_body(c, carry):
        for lanes in head_lanes:
            _stage_head(chunk_rows(c), lanes, q_scr, k_scr, v_scr, lf_scr, c,
                        stage)
        return carry

    def combine_body(c, carry):
        rows = chunk_rows(c)
        row_s = lax.broadcasted_iota(jnp.int32, (SCAN_CHUNK, 2 * SCAN_CHUNK), 0)
        col_s = lax.broadcasted_iota(
            jnp.int32, (SCAN_CHUNK, 2 * SCAN_CHUNK), 1) & (SCAN_CHUNK - 1)
        level_of_pair = {
            half: (((row_s ^ col_s) >> (half.bit_length() - 1)) == 1)
            & (row_s > col_s) for half in SCAN_HALVES[1:]}
        for pair in range(HG_HEADS // 2):
            out = _combine_head_pair(pair, rows, c, level_of_pair, stage,
                                     state_scr)
            for hd in (2 * pair, 2 * pair + 1):
                o = out[:, (hd % 2) * HG_DIM:(hd % 2 + 1) * HG_DIM]
                ms = jnp.mean(o * o, axis=-1, keepdims=True)
                o_scr[rows, head_lanes[hd]] = o * lax.rsqrt(ms + EPS)
        return carry

    def chunk_body(c, carry):
        return combine_body(c, stage_body(c, carry))

    lax.fori_loop(0, tm // SCAN_CHUNK, chunk_body, 0)

    gated = (o_scr[...] * gn_ref[...]) * sg_scr[...]
    y = jnp.dot(gated.astype(BF16), wout_ref[...], preferred_element_type=F32)
    o_ref[...] = x + mod[2:3] * y


def _hgrn_layer(x2, mod, norm_g, w_in, lb_raw, layer, gn_g, w_out, batch, seq):
    n, d = x2.shape
    tm = TOKEN_TILE
    tiles_per_seq = seq // tm
    act = pltpu.VMEM((tm, d), F32)
    return pl.pallas_call(
        functools.partial(_hgrn_kernel, layer=layer),
        grid=(batch, tiles_per_seq),
        in_specs=[
            pl.BlockSpec((tm, d), lambda b, t: (b * tiles_per_seq + t, 0)),
            pl.BlockSpec((1, 6, d), lambda b, t: (b, 0, 0)),
            _resident((1, d)),
            _resident(w_in.shape),
            _resident(lb_raw.shape),
            _resident((1, d)),
            _resident(w_out.shape),
        ],
        out_specs=pl.BlockSpec((tm, d), lambda b, t: (b * tiles_per_seq + t, 0)),
        out_shape=jax.ShapeDtypeStruct((n, d), F32),
        scratch_shapes=[
            pltpu.VMEM((tm, d), BF16),
            act, act, act, act, act, act,
            pltpu.VMEM((HG_HEADS, HG_DIM, HG_DIM), F32),
            pltpu.VMEM((len(SCAN_HALVES), tm, d), BF16),
            pltpu.VMEM((len(SCAN_HALVES), tm, d), BF16),
            pltpu.VMEM((tm, d), BF16),
            pltpu.VMEM((tm, d), BF16),
            pltpu.VMEM((tm, d), BF16),
            pltpu.VMEM((tm // SCAN_CHUNK, 1, d), F32),
        ],
        compiler_params=pltpu.CompilerParams(
            dimension_semantics=("arbitrary", "arbitrary"),
            vmem_limit_bytes=VMEM_LIMIT_BYTES),
        name="hgrn_mixer",
    )(x2, mod, norm_g.reshape(1, d), w_in, lb_raw.astype(F32),
      gn_g.reshape(1, d), w_out)


def _ffn_kernel(x_ref, mod_ref, ng_ref, wup_ref, cw_ref, cb_ref, wdown_ref,
                fg_ref, o_ref, h_scr, p_scr, halo_scr, *, final_norm):
    tm, d = x_ref.shape
    hidden = wdown_ref.shape[0]

    @pl.when(pl.program_id(1) == 0)
    def _():
        halo_scr[...] = jnp.zeros_like(halo_scr)

    x = x_ref[...]
    mod = mod_ref[0]
    h = _modulated_rms_norm(x, ng_ref[...], mod[4:5], mod[3:4])
    h_scr[...] = h.astype(BF16)

    cn = 256
    row = lax.broadcasted_iota(jnp.int32, (tm, cn), 0)

    def conv(cols):
        a = jnp.dot(h_scr[...], wup_ref[:, cols], preferred_element_type=F32)
        prev1 = halo_scr[1:2, cols]
        prev2 = halo_scr[0:1, cols]
        halo_scr[:, cols] = a[tm - 2:tm, :]
        a1 = jnp.where(row == 0, prev1, pltpu.roll(a, 1, 0))
        a2 = jnp.where(row == 0, prev2,
                       jnp.where(row == 1, prev1, pltpu.roll(a, 2, 0)))
        return (cb_ref[:, cols] + cw_ref[0:1, cols] * a2
                + cw_ref[1:2, cols] * a1 + cw_ref[2:3, cols] * a)

    for j in range(hidden // cn):
        gate = conv(slice(j * cn, (j + 1) * cn))
        val = conv(slice(hidden + j * cn, hidden + (j + 1) * cn))
        p_scr[:, j * cn:(j + 1) * cn] = (_gelu(gate) * val).astype(BF16)

    y = jnp.dot(p_scr[...], wdown_ref[...], preferred_element_type=F32)
    out = x + mod[5:6] * y
    if final_norm:
        ms = jnp.mean(out * out, axis=-1, keepdims=True)
        out = (out * lax.rsqrt(ms + EPS)) * fg_ref[...]
    o_ref[...] = out


def _ffn_layer(x2, mod, norm_g, w_up, conv_w, conv_b, w_down, final_g, batch,
               seq, final_norm):
    n, d = x2.shape
    tm = TOKEN_TILE
    tiles_per_seq = seq // tm
    hidden = w_down.shape[0]
    return pl.pallas_call(
        functools.partial(_ffn_kernel, final_norm=final_norm),
        grid=(batch, tiles_per_seq),
        in_specs=[
            pl.BlockSpec((tm, d), lambda b, t: (b * tiles_per_seq + t, 0)),
            pl.BlockSpec((1, 6, d), lambda b, t: (b, 0, 0)),
            _resident((1, d)),
            _resident(w_up.shape),
            _resident(conv_w.shape),
            _resident((1, 2 * hidden)),
            _resident(w_down.shape),
            _resident((1, d)),
        ],
        out_specs=pl.BlockSpec((tm, d), lambda b, t: (b * tiles_per_seq + t, 0)),
        out_shape=jax.ShapeDtypeStruct((n, d), F32),
        scratch_shapes=[
            pltpu.VMEM((tm, d), BF16),
            pltpu.VMEM((tm, hidden), BF16),
            pltpu.VMEM((CONV_WIDTH - 1, 2 * hidden), F32),
        ],
        compiler_params=pltpu.CompilerParams(
            dimension_semantics=("arbitrary", "arbitrary"),
            vmem_limit_bytes=VMEM_LIMIT_BYTES),
        name="conv_ffn",
    )(x2, mod, norm_g.reshape(1, d), w_up, conv_w, conv_b.reshape(1, 2 * hidden),
      w_down, final_g.reshape(1, d))


def kernel(x, c, gm_w_in, gm_ln_g, gm_ln_b, gm_w_s, gm_b_s, gm_w_out, hg_w_in, hg_lb, hg_gn_g, hg_w_out, ffn_w_up, ffn_conv_w, ffn_conv_b, ffn_w_down, norm_g, ada_w, ada_b, final_g):
    batch, seq, d = x.shape
    depth = ada_w.shape[0]
    assert seq % TOKEN_TILE == 0 and TOKEN_TILE % GM_BLOCK == 0

    c_pad = jnp.pad(c, ((0, 8 - batch), (0, 0)))
    mod_all = _ada_modulation(c_pad, ada_w, ada_b)
    mod_all = mod_all[:, :batch].reshape(depth, batch, 6, d)

    head_dim = gm_w_out.shape[1] // GM_HEADS
    x2 = x.reshape(batch * seq, d)
    for i in range(depth):
        j = i // 2
        if i % 2 == 0:
            b_s_tile = jnp.repeat(gm_b_s[j].T, head_dim, axis=1)
            x2 = _gmlp_layer(x2, mod_all[i], norm_g[i, 0],
                             gm_w_in[j].astype(BF16), gm_ln_g[j], gm_ln_b[j],
                             gm_w_s[j], b_s_tile, gm_w_out[j].astype(BF16), seq)
        else:
            x2 = _hgrn_layer(x2, mod_all[i], norm_g[i, 0],
                             hg_w_in[j].astype(BF16), hg_lb, j, hg_gn_g[j],
                             hg_w_out[j].astype(BF16), batch, seq)
        x2 = _ffn_layer(x2, mod_all[i], norm_g[i, 1], ffn_w_up[i].astype(BF16),
                        ffn_conv_w[i], ffn_conv_b[i],
                        ffn_w_down[i].astype(BF16), final_g, batch, seq,
                        final_norm=(i == depth - 1))
    return x2.reshape(batch, seq, d)
```

```python
import functools

import jax
import jax.numpy as jnp
from jax import lax
from jax.experimental import pallas as pl
from jax.experimental.pallas import tpu as pltpu

F32 = jnp.float32
BF16 = jnp.bfloat16

EPS = 1e-6
CHUNK = 64
GM_BLOCK = 128
GM_HEADS = 8
HG_HEADS = 8
HG_DIM = 128
SCAN_CHUNK = 64
SCAN_HALVES = (32, 16, 8, 4, 2, 1)
PIPELINE_DEPTH = 2
VREG_ROWS = 8
LOG2_E = 1.4426950408889634
CONV_WIDTH = 3
SQRT_HALF = 0.7071067811865476

TOKEN_TILE = 512
VMEM_LIMIT_BYTES = 56 * 1024 * 1024


def _gelu(x):
    return 0.5 * x * (1.0 + lax.erf(x * SQRT_HALF))


def _silu(x):
    return x * jax.nn.sigmoid(x)


def _modulated_rms_norm(x, gain, scale, shift):
    ms = jnp.mean(x * x, axis=-1, keepdims=True)
    return (x * lax.rsqrt(ms + EPS)) * (gain * (1.0 + scale)) + shift


def _resident(shape):
    zeros = (0,) * len(shape)
    return pl.BlockSpec(shape, lambda *_: zeros, pipeline_mode=pl.Buffered(1))


def _ada_kernel(c_ref, w_ref, b_ref, o_ref):
    cond = _silu(c_ref[...])
    o_ref[0] = jnp.dot(cond, w_ref[0], preferred_element_type=F32) + b_ref[0]


def _ada_modulation(c_pad, ada_w, ada_b):
    depth, d, n = ada_w.shape
    rows = c_pad.shape[0]
    tn = 1536
    return pl.pallas_call(
        _ada_kernel,
        grid=(depth, n // tn),
        in_specs=[
            pl.BlockSpec((rows, d), lambda l, j: (0, 0)),
            pl.BlockSpec((1, d, tn), lambda l, j: (l, 0, j)),
            pl.BlockSpec((1, 1, tn), lambda l, j: (l, 0, j)),
        ],
        out_specs=pl.BlockSpec((1, rows, tn), lambda l, j: (l, 0, j)),
        out_shape=jax.ShapeDtypeStruct((depth, rows, n), F32),
        compiler_params=pltpu.CompilerParams(
            dimension_semantics=("arbitrary", "arbitrary"),
            vmem_limit_bytes=VMEM_LIMIT_BYTES),
        name="ada_modulation",
    )(c_pad, ada_w, ada_b.reshape(depth, 1, n))


def _gmlp_kernel(x_ref, mod_ref, ng_ref, win_ref, lng_ref, lnb_ref, ws_ref,
                 bs_ref, wout_ref, o_ref, h_scr, u_scr, v_scr, vn_scr, gt_scr):
    tm, d = x_ref.shape
    width = u_scr.shape[1]
    head_dim = width // GM_HEADS
    x = x_ref[...]
    mod = mod_ref[0]
    h = _modulated_rms_norm(x, ng_ref[...], mod[1:2], mod[0:1])
    h_scr[...] = h.astype(BF16)

    cn = 512
    for j in range(2 * width // cn):
        z = jnp.dot(h_scr[...], win_ref[:, j * cn:(j + 1) * cn],
                    preferred_element_type=F32)
        z = _gelu(z)
        if j * cn < width:
            u_scr[:, j * cn:(j + 1) * cn] = z
        else:
            v_scr[:, j * cn - width:(j + 1) * cn - width] = z

    v = v_scr[...]
    mu = jnp.mean(v, axis=-1, keepdims=True)
    dv = v - mu
    var = jnp.mean(dv * dv, axis=-1, keepdims=True)
    vn = dv * lax.rsqrt(var + EPS) * lng_ref[...] + lnb_ref[...]
    vn_scr[...] = vn.astype(BF16)

    row = lax.broadcasted_iota(jnp.int32, (GM_BLOCK, GM_BLOCK), 0)
    col = lax.broadcasted_iota(jnp.int32, (GM_BLOCK, GM_BLOCK), 1)
    allowed = (row >= CHUNK) | (col < CHUNK)
    for hd in range(GM_HEADS):
        w = jnp.where(allowed, ws_ref[hd], 0.0).astype(BF16)
        cols = slice(hd * head_dim, (hd + 1) * head_dim)
        for r in range(tm // GM_BLOCK):
            rows = slice(r * GM_BLOCK, (r + 1) * GM_BLOCK)
            s = jnp.dot(w, vn_scr[rows, cols], preferred_element_type=F32)
            s = s + bs_ref[:, cols]
            gt_scr[rows, cols] = (u_scr[rows, cols] * s).astype(BF16)

    y = jnp.dot(gt_scr[...], wout_ref[...], preferred_element_type=F32)
    o_ref[...] = x + mod[2:3] * y


def _gmlp_layer(x2, mod, norm_g, w_in, ln_g, ln_b, w_s, b_s_tile, w_out, seq):
    n, d = x2.shape
    tm = TOKEN_TILE
    width = w_out.shape[0]
    tiles_per_seq = seq // tm
    return pl.pallas_call(
        _gmlp_kernel,
        grid=(n // tm,),
        in_specs=[
            pl.BlockSpec((tm, d), lambda t: (t, 0)),
            pl.BlockSpec((1, 6, d), lambda t: (t // tiles_per_seq, 0, 0)),
            _resident((1, d)),
            _resident(w_in.shape),
            _resident((1, width)),
            _resident((1, width)),
            _resident(w_s.shape),
            _resident(b_s_tile.shape),
            _resident(w_out.shape),
        ],
        out_specs=pl.BlockSpec((tm, d), lambda t: (t, 0)),
        out_shape=jax.ShapeDtypeStruct((n, d), F32),
        scratch_shapes=[
            pltpu.VMEM((tm, d), BF16),
            pltpu.VMEM((tm, width), F32),
            pltpu.VMEM((tm, width), F32),
            pltpu.VMEM((tm, width), BF16),
            pltpu.VMEM((tm, width), BF16),
        ],
        compiler_params=pltpu.CompilerParams(
            dimension_semantics=("arbitrary",),
            vmem_limit_bytes=VMEM_LIMIT_BYTES),
        name="gmlp_mixer",
    )(x2, mod, norm_g.reshape(1, d), w_in, ln_g.reshape(1, width),
      ln_b.reshape(1, width), w_s, b_s_tile, w_out)


def _chunk_cumsum(g):
    groups = SCAN_CHUNK // VREG_ROWS
    g3 = g.reshape(groups, VREG_ROWS, g.shape[1])
    sub = lax.broadcasted_iota(jnp.int32, g3.shape, 1)
    step = 1
    while step < VREG_ROWS:
        g3 = g3 + jnp.where(sub >= step, pltpu.roll(g3, step, 1), 0.0)
        step *= 2
    pieces = [g3[0]]
    for r in range(1, groups):
        pieces.append(g3[r] + pieces[-1][VREG_ROWS - 1:VREG_ROWS, :])
    return jnp.concatenate(pieces, axis=0)


def _split_operands(q, k, cum, g, half):
    lanes = q.shape[1]
    if half >= VREG_ROWS:
        zeros = jnp.zeros((half, lanes), F32)
        q_parts, k_parts = [], []
        for start in range(0, SCAN_CHUNK, 2 * half):
            mid = start + half
            bnd = cum[mid - 1:mid, :]
            k_parts += [k[start:mid] * jnp.exp2(bnd - cum[start:mid]), zeros]
            q_parts += [zeros,
                        q[mid:mid + half] * jnp.exp2(cum[mid:mid + half] - bnd)]
        return jnp.concatenate(q_parts, axis=0), jnp.concatenate(k_parts, axis=0)
    if half == 1:
        return q * jnp.exp2(g), k
    groups = SCAN_CHUNK // VREG_ROWS
    cum3 = cum.reshape(groups, VREG_ROWS, lanes)
    sub = lax.broadcasted_iota(jnp.int32, cum3.shape, 1)
    bnd = None
    for start in range(0, VREG_ROWS, 2 * half):
        piece = jnp.broadcast_to(
            cum3[:, start + half - 1:start + half, :], cum3.shape)
        bnd = piece if bnd is None else jnp.where(sub >= start, piece, bnd)
    e = jnp.exp2(-jnp.abs(cum3 - bnd)).reshape(SCAN_CHUNK, lanes)
    return q * e, k * e


def _stage_head(rows, lanes, q_ref, k_ref, v_ref, g_ref, chunk, stage):
    ql_ref, ku_ref, qin_ref, kout_ref, v16_ref, direct_ref, decay_ref = stage
    q, k, v, g = (ref[rows, lanes] for ref in (q_ref, k_ref, v_ref, g_ref))
    cum = _chunk_cumsum(g)
    last = cum[SCAN_CHUNK - 1:SCAN_CHUNK, :]
    qin_ref[rows, lanes] = (q * jnp.exp2(cum)).astype(BF16)
    kout_ref[rows, lanes] = (k * jnp.exp2(last - cum)).astype(BF16)
    v16_ref[rows, lanes] = v.astype(BF16)
    decay_ref[chunk, :, lanes] = jnp.exp2(last)
    direct_ref[rows, lanes] = jnp.sum(q * k, axis=-1, keepdims=True) * v
    for idx, half in enumerate(SCAN_HALVES):
        q_l, k_u = _split_operands(q, k, cum, g, half)
        ql_ref[idx, rows, lanes] = q_l.astype(BF16)
        ku_ref[idx, rows, lanes] = k_u.astype(BF16)


def _block_diag(a, b):
    zeros = jnp.zeros_like(a)
    return jnp.concatenate([jnp.concatenate([a, zeros], axis=1),
                            jnp.concatenate([zeros, b], axis=1)], axis=0)


def _combine_head_pair(pair, rows, chunk, level_of_pair, stage, state_ref):
    ql_ref, ku_ref, qin_ref, kout_ref, v16_ref, direct_ref, decay_ref = stage
    nt = (((1,), (1,)), ((), ()))
    first, second = 2 * pair, 2 * pair + 1
    lanes_a = slice(first * HG_DIM, second * HG_DIM)
    lanes_b = slice(second * HG_DIM, (second + 1) * HG_DIM)
    lanes = slice(first * HG_DIM, (second + 1) * HG_DIM)

    state_a, state_b = state_ref[first], state_ref[second]
    out = direct_ref[rows, lanes] + lax.dot_general(
        qin_ref[rows, lanes],
        _block_diag(state_a.astype(BF16), state_b.astype(BF16)), nt,
        preferred_element_type=F32)
    v_bd = _block_diag(v16_ref[rows, lanes_a], v16_ref[rows, lanes_b])
    update = lax.dot_general(
        v_bd, _block_diag(kout_ref[rows, lanes_a], kout_ref[rows, lanes_b]),
        (((0,), (0,)), ((), ())), preferred_element_type=F32)
    state_ref[first] = (state_a * decay_ref[chunk, :, lanes_a]
                        + update[:HG_DIM, :HG_DIM])
    state_ref[second] = (state_b * decay_ref[chunk, :, lanes_b]
                         + update[HG_DIM:, HG_DIM:])
    level_scores = [
        lax.dot_general(
            ql_ref[idx, rows, lanes],
            _block_diag(ku_ref[idx, rows, lanes_a], ku_ref[idx, rows, lanes_b]),
            nt, preferred_element_type=F32)
        for idx in range(len(SCAN_HALVES))]

    def finish():
        scores = level_scores[0]
        for half, s in zip(SCAN_HALVES[1:], level_scores[1:]):
            scores = jnp.where(level_of_pair[half], s, scores)
        return out + jnp.dot(scores.astype(BF16), v_bd,
                             preferred_element_type=F32)

    return finish


def _hgrn_lower_bound(lb_raw, layer):
    e = jnp.exp(lb_raw - jnp.max(lb_raw, axis=0, keepdims=True))
    total = jnp.sum(e, axis=0, keepdims=True)
    below = jnp.zeros_like(total)
    for m in range(1, layer + 1):
        below = below + e[m:m + 1]
    return below / total


def _hgrn_kernel(x_ref, mod_ref, ng_ref, win_ref, lb_ref, gn_ref, wout_ref,
                 o_ref, h_scr, q_scr, k_scr, v_scr, lf_scr, sg_scr, o_scr,
                 state_scr, ql_scr, ku_scr, qin_scr, kout_scr, v16_scr,
                 decay_scr, *, layer):
    tm, d = x_ref.shape
    stage = (ql_scr, ku_scr, qin_scr, kout_scr, v16_scr, o_scr, decay_scr)

    @pl.when(pl.program_id(1) == 0)
    def _():
        state_scr[...] = jnp.zeros_like(state_scr)

    x = x_ref[...]
    mod = mod_ref[0]
    h = _modulated_rms_norm(x, ng_ref[...], mod[1:2], mod[0:1])
    h_scr[...] = h.astype(BF16)

    def proj(i):
        return jnp.dot(h_scr[...], win_ref[:, i * d:(i + 1) * d],
                       preferred_element_type=F32)

    q_scr[...] = _silu(proj(0))
    lb = _hgrn_lower_bound(lb_ref[...], layer)
    f = lb + (1.0 - lb) * jax.nn.sigmoid(proj(1))
    lf_scr[...] = jnp.log(f) * LOG2_E
    k_scr[...] = 1.0 - f
    v_scr[...] = proj(2)
    sg_scr[...] = _silu(proj(3))

    head_lanes = [slice(hd * HG_DIM, (hd + 1) * HG_DIM)
                  for hd in range(HG_HEADS)]

    def chunk_rows(c):
        return pl.ds(pl.multiple_of(c * SCAN_CHUNK, SCAN_CHUNK), SCAN_CHUNK)

    def finish_pair(rows, pair, finish):
        out = finish()
        for hd in (2 * pair, 2 * pair + 1):
            o = out[:, (hd % 2) * HG_DIM:(hd % 2 + 1) * HG_DIM]
            ms = jnp.mean(o * o, axis=-1, keepdims=True)
            o_scr[rows, head_lanes[hd]] = o * lax.rsqrt(ms + EPS)

    def chunk_body(c, carry):
        rows = chunk_rows(c)
        row_s = lax.broadcasted_iota(jnp.int32, (SCAN_CHUNK, 2 * SCAN_CHUNK), 0)
        col_s = lax.broadcasted_iota(
            jnp.int32, (SCAN_CHUNK, 2 * SCAN_CHUNK), 1) & (SCAN_CHUNK - 1)
        level_of_pair = {
            half: (((row_s ^ col_s) >> (half.bit_length() - 1)) == 1)
            & (row_s > col_s) for half in SCAN_HALVES[1:]}
        pending = []
        for pair in range(HG_HEADS // 2):
            for hd in (2 * pair, 2 * pair + 1):
                _stage_head(rows, head_lanes[hd], q_scr, k_scr, v_scr, lf_scr,
                            c, stage)
            finish = _combine_head_pair(pair, rows, c, level_of_pair, stage,
                                        state_scr)
            pending.append((rows, pair, finish))
            if len(pending) > PIPELINE_DEPTH:
                finish_pair(*pending.pop(0))
        for item in pending:
            finish_pair(*item)
        return carry

    lax.fori_loop(0, tm // SCAN_CHUNK, chunk_body, 0)

    gated = (o_scr[...] * gn_ref[...]) * sg_scr[...]
    y = jnp.dot(gated.astype(BF16), wout_ref[...], preferred_element_type=F32)
    o_ref[...] = x + mod[2:3] * y


def _hgrn_layer(x2, mod, norm_g, w_in, lb_raw, layer, gn_g, w_out, batch, seq):
    n, d = x2.shape
    tm = TOKEN_TILE
    tiles_per_seq = seq // tm
    act = pltpu.VMEM((tm, d), F32)
    return pl.pallas_call(
        functools.partial(_hgrn_kernel, layer=layer),
        grid=(batch, tiles_per_seq),
        in_specs=[
            pl.BlockSpec((tm, d), lambda b, t: (b * tiles_per_seq + t, 0)),
            pl.BlockSpec((1, 6, d), lambda b, t: (b, 0, 0)),
            _resident((1, d)),
            _resident(w_in.shape),
            _resident(lb_raw.shape),
            _resident((1, d)),
            _resident(w_out.shape),
        ],
        out_specs=pl.BlockSpec((tm, d), lambda b, t: (b * tiles_per_seq + t, 0)),
        out_shape=jax.ShapeDtypeStruct((n, d), F32),
        scratch_shapes=[
            pltpu.VMEM((tm, d), BF16),
            act, act, act, act, act, act,
            pltpu.VMEM((HG_HEADS, HG_DIM, HG_DIM), F32),
            pltpu.VMEM((len(SCAN_HALVES), tm, d), BF16),
            pltpu.VMEM((len(SCAN_HALVES), tm, d), BF16),
            pltpu.VMEM((tm, d), BF16),
            pltpu.VMEM((tm, d), BF16),
            pltpu.VMEM((tm, d), BF16),
            pltpu.VMEM((tm // SCAN_CHUNK, 1, d), F32),
        ],
        compiler_params=pltpu.CompilerParams(
            dimension_semantics=("arbitrary", "arbitrary"),
            vmem_limit_bytes=VMEM_LIMIT_BYTES),
        name="hgrn_mixer",
    )(x2, mod, norm_g.reshape(1, d), w_in, lb_raw.astype(F32),
      gn_g.reshape(1, d), w_out)


def _ffn_kernel(x_ref, mod_ref, ng_ref, wup_ref, cw_ref, cb_ref, wdown_ref,
                fg_ref, o_ref, h_scr, p_scr, halo_scr, *, final_norm):
    tm, d = x_ref.shape
    hidden = wdown_ref.shape[0]

    @pl.when(pl.program_id(1) == 0)
    def _():
        halo_scr[...] = jnp.zeros_like(halo_scr)

    x = x_ref[...]
    mod = mod_ref[0]
    h = _modulated_rms_norm(x, ng_ref[...], mod[4:5], mod[3:4])
    h_scr[...] = h.astype(BF16)

    cn = 256
    row = lax.broadcasted_iota(jnp.int32, (tm, cn), 0)

    def conv(cols):
        a = jnp.dot(h_scr[...], wup_ref[:, cols], preferred_element_type=F32)
        prev1 = halo_scr[1:2, cols]
        prev2 = halo_scr[0:1, cols]
        halo_scr[:, cols] = a[tm - 2:tm, :]
        a1 = jnp.where(row == 0, prev1, pltpu.roll(a, 1, 0))
        a2 = jnp.where(row == 0, prev2,
                       jnp.where(row == 1, prev1, pltpu.roll(a, 2, 0)))
        return (cb_ref[:, cols] + cw_ref[0:1, cols] * a2
                + cw_ref[1:2, cols] * a1 + cw_ref[2:3, cols] * a)

    for j in range(hidden // cn):
        gate = conv(slice(j * cn, (j + 1) * cn))
        val = conv(slice(hidden + j * cn, hidden + (j + 1) * cn))
        p_scr[:, j * cn:(j + 1) * cn] = (_gelu(gate) * val).astype(BF16)

    y = jnp.dot(p_scr[...], wdown_ref[...], preferred_element_type=F32)
    out = x + mod[5:6] * y
    if final_norm:
        ms = jnp.mean(out * out, axis=-1, keepdims=True)
        out = (out * lax.rsqrt(ms + EPS)) * fg_ref[...]
    o_ref[...] = out


def _ffn_layer(x2, mod, norm_g, w_up, conv_w, conv_b, w_down, final_g, batch,
               seq, final_norm):
    n, d = x2.shape
    tm = TOKEN_TILE
    tiles_per_seq = seq // tm
    hidden = w_down.shape[0]
    return pl.pallas_call(
        functools.partial(_ffn_kernel, final_norm=final_norm),
        grid=(batch, tiles_per_seq),
        in_specs=[
            pl.BlockSpec((tm, d), lambda b, t: (b * tiles_per_seq + t, 0)),
            pl.BlockSpec((1, 6, d), lambda b, t: (b, 0, 0)),
            _resident((1, d)),
            _resident(w_up.shape),
            _resident(conv_w.shape),
            _resident((1, 2 * hidden)),
            _resident(w_down.shape),
            _resident((1, d)),
        ],
        out_specs=pl.BlockSpec((tm, d), lambda b, t: (b * tiles_per_seq + t, 0)),
        out_shape=jax.ShapeDtypeStruct((n, d), F32),
        scratch_shapes=[
            pltpu.VMEM((tm, d), BF16),
            pltpu.VMEM((tm, hidden), BF16),
            pltpu.VMEM((CONV_WIDTH - 1, 2 * hidden), F32),
        ],
        compiler_params=pltpu.CompilerParams(
            dimension_semantics=("arbitrary", "arbitrary"),
            vmem_limit_bytes=VMEM_LIMIT_BYTES),
        name="conv_ffn",
    )(x2, mod, norm_g.reshape(1, d), w_up, conv_w, conv_b.reshape(1, 2 * hidden),
      w_down, final_g.reshape(1, d))


def kernel(x, c, gm_w_in, gm_ln_g, gm_ln_b, gm_w_s, gm_b_s, gm_w_out, hg_w_in, hg_lb, hg_gn_g, hg_w_out, ffn_w_up, ffn_conv_w, ffn_conv_b, ffn_w_down, norm_g, ada_w, ada_b, final_g):
    batch, seq, d = x.shape
    depth = ada_w.shape[0]
    assert seq % TOKEN_TILE == 0 and TOKEN_TILE % GM_BLOCK == 0

    c_pad = jnp.pad(c, ((0, 8 - batch), (0, 0)))
    mod_all = _ada_modulation(c_pad, ada_w, ada_b)
    mod_all = mod_all[:, :batch].reshape(depth, batch, 6, d)

    head_dim = gm_w_out.shape[1] // GM_HEADS
    x2 = x.reshape(batch * seq, d)
    for i in range(depth):
        j = i // 2
        if i % 2 == 0:
            b_s_tile = jnp.repeat(gm_b_s[j].T, head_dim, axis=1)
            x2 = _gmlp_layer(x2, mod_all[i], norm_g[i, 0],
                             gm_w_in[j].astype(BF16), gm_ln_g[j], gm_ln_b[j],
                             gm_w_s[j], b_s_tile, gm_w_out[j].astype(BF16), seq)
        else:
            x2 = _hgrn_layer(x2, mod_all[i], norm_g[i, 0],
                             hg_w_in[j].astype(BF16), hg_lb, j, hg_gn_g[j],
                             hg_w_out[j].astype(BF16), batch, seq)
        x2 = _ffn_layer(x2, mod_all[i], norm_g[i, 1], ffn_w_up[i].astype(BF16),
                        ffn_conv_w[i], ffn_conv_b[i],
                        ffn_w_down[i].astype(BF16), final_g, batch, seq,
                        final_norm=(i == depth - 1))
    return x2.reshape(batch, seq, d)
```

```python
import functools

import jax
import jax.numpy as jnp
from jax import lax
from jax.experimental import pallas as pl
from jax.experimental.pallas import tpu as pltpu

F32 = jnp.float32
BF16 = jnp.bfloat16

EPS = 1e-6
CHUNK = 64
GM_BLOCK = 128
GM_HEADS = 8
HG_HEADS = 8
HG_DIM = 128
SCAN_CHUNK = 64
SCAN_HALVES = (32, 16, 8, 4, 2, 1)
SCAN_SCHEDULE = "F0 F1 F2 F3 M0 M1 B0 U0 M2 B1 U1 M3 B2 U2 U3 B3"
VREG_ROWS = 8
LOG2_E = 1.4426950408889634
CONV_WIDTH = 3
FFN_BLOCK = 256
SQRT_HALF = 0.7071067811865476

TOKEN_TILE = 512
VMEM_LIMIT_BYTES = 56 * 1024 * 1024


def _gelu(x):
    return 0.5 * x * (1.0 + lax.erf(x * SQRT_HALF))


def _silu(x):
    return x * (0.5 * jnp.tanh(0.5 * x) + 0.5)


def _modulated_rms_norm(x, gain, scale, shift):
    ms = jnp.mean(x * x, axis=-1, keepdims=True)
    return (x * lax.rsqrt(ms + EPS)) * (gain * (1.0 + scale)) + shift


def _resident(shape):
    zeros = (0,) * len(shape)
    return pl.BlockSpec(shape, lambda *_: zeros, pipeline_mode=pl.Buffered(1))


def _ada_kernel(c_ref, w_ref, b_ref, o_ref):
    cond = _silu(c_ref[...])
    o_ref[0] = jnp.dot(cond, w_ref[0], preferred_element_type=F32) + b_ref[0]


def _ada_modulation(c_pad, ada_w, ada_b):
    depth, d, n = ada_w.shape
    rows = c_pad.shape[0]
    tn = 1536
    return pl.pallas_call(
        _ada_kernel,
        grid=(depth, n // tn),
        in_specs=[
            pl.BlockSpec((rows, d), lambda l, j: (0, 0)),
            pl.BlockSpec((1, d, tn), lambda l, j: (l, 0, j)),
            pl.BlockSpec((1, 1, tn), lambda l, j: (l, 0, j)),
        ],
        out_specs=pl.BlockSpec((1, rows, tn), lambda l, j: (l, 0, j)),
        out_shape=jax.ShapeDtypeStruct((depth, rows, n), F32),
        compiler_params=pltpu.CompilerParams(
            dimension_semantics=("arbitrary", "arbitrary"),
            vmem_limit_bytes=VMEM_LIMIT_BYTES),
        name="ada_modulation",
    )(c_pad, ada_w, ada_b.reshape(depth, 1, n))


def _gmlp_kernel(x_ref, mod_ref, ng_ref, win_ref, lng_ref, lnb_ref, ws_ref,
                 bs_ref, wout_ref, o_ref, h_scr, u_scr, v_scr, vn_scr, gt_scr):
    tm, d = x_ref.shape
    width = u_scr.shape[1]
    head_dim = width // GM_HEADS
    x = x_ref[...]
    mod = mod_ref[0]
    h = _modulated_rms_norm(x, ng_ref[...], mod[1:2], mod[0:1])
    h_scr[...] = h.astype(BF16)

    cn = 512

    def store_gelu(dst_ref, j, z):
        dst_ref[:, j * cn:(j + 1) * cn] = _gelu(z)

    def layer_norm_v():
        v = v_scr[...]
        mu = jnp.mean(v, axis=-1, keepdims=True)
        dv = v - mu
        var = jnp.mean(dv * dv, axis=-1, keepdims=True)
        vn = dv * lax.rsqrt(var + EPS) * lng_ref[...] + lnb_ref[...]
        vn_scr[...] = vn.astype(BF16)

    row = lax.broadcasted_iota(jnp.int32, (GM_BLOCK, GM_BLOCK), 0)
    col = lax.broadcasted_iota(jnp.int32, (GM_BLOCK, GM_BLOCK), 1)
    allowed = (row >= CHUNK) | (col < CHUNK)

    def gate(hd):
        w = jnp.where(allowed, ws_ref[hd], 0.0).astype(BF16)
        cols = slice(hd * head_dim, (hd + 1) * head_dim)
        for r in range(tm // GM_BLOCK):
            rows = slice(r * GM_BLOCK, (r + 1) * GM_BLOCK)
            s = jnp.dot(w, vn_scr[rows, cols], preferred_element_type=F32)
            s = s + bs_ref[:, cols]
            gt_scr[rows, cols] = (u_scr[rows, cols] * s).astype(BF16)

    blocks = width // cn
    heads_per_block = cn // head_dim
    elementwise = []
    for dst_ref, first_col in ((v_scr, width), (u_scr, 0)):
        for j in range(blocks):
            cols = slice(first_col + j * cn, first_col + (j + 1) * cn)
            z = jnp.dot(h_scr[...], win_ref[:, cols], preferred_element_type=F32)
            for work in elementwise:
                work()
            elementwise = [functools.partial(store_gelu, dst_ref, j, z)]
            if dst_ref is v_scr and j == blocks - 1:
                elementwise.append(layer_norm_v)
            if dst_ref is u_scr:
                elementwise += [functools.partial(gate, hd) for hd in range(
                    j * heads_per_block, (j + 1) * heads_per_block)]
    for work in elementwise:
        work()
    y = jnp.dot(gt_scr[...], wout_ref[...], preferred_element_type=F32)
    o_ref[...] = x + mod[2:3] * y


def _gmlp_layer(x2, mod, norm_g, w_in, ln_g, ln_b, w_s, b_s_tile, w_out, seq):
    n, d = x2.shape
    tm = TOKEN_TILE
    width = w_out.shape[0]
    tiles_per_seq = seq // tm
    return pl.pallas_call(
        _gmlp_kernel,
        grid=(n // tm,),
        in_specs=[
            pl.BlockSpec((tm, d), lambda t: (t, 0)),
            pl.BlockSpec((1, 6, d), lambda t: (t // tiles_per_seq, 0, 0)),
            _resident((1, d)),
            _resident(w_in.shape),
            _resident((1, width)),
            _resident((1, width)),
            _resident(w_s.shape),
            _resident(b_s_tile.shape),
            _resident(w_out.shape),
        ],
        out_specs=pl.BlockSpec((tm, d), lambda t: (t, 0)),
        out_shape=jax.ShapeDtypeStruct((n, d), F32),
        scratch_shapes=[
            pltpu.VMEM((tm, d), BF16),
            pltpu.VMEM((tm, width), F32),
            pltpu.VMEM((tm, width), F32),
            pltpu.VMEM((tm, width), BF16),
            pltpu.VMEM((tm, width), BF16),
        ],
        compiler_params=pltpu.CompilerParams(
            dimension_semantics=("arbitrary",),
            vmem_limit_bytes=VMEM_LIMIT_BYTES),
        name="gmlp_mixer",
    )(x2, mod, norm_g.reshape(1, d), w_in, ln_g.reshape(1, width),
      ln_b.reshape(1, width), w_s, b_s_tile, w_out)


def _chunk_cumsum(g):
    groups = SCAN_CHUNK // VREG_ROWS
    g3 = g.reshape(groups, VREG_ROWS, g.shape[1])
    sub = lax.broadcasted_iota(jnp.int32, g3.shape, 1)
    step = 1
    while step < VREG_ROWS:
        g3 = g3 + jnp.where(sub >= step, pltpu.roll(g3, step, 1), 0.0)
        step *= 2
    pieces = [g3[0]]
    for r in range(1, groups):
        pieces.append(g3[r] + pieces[-1][VREG_ROWS - 1:VREG_ROWS, :])
    return jnp.concatenate(pieces, axis=0)


def _split_operands(q, k, cum, g, half):
    lanes = q.shape[1]
    if half >= VREG_ROWS:
        zeros = jnp.zeros((half, lanes), F32)
        q_parts, k_parts = [], []
        for start in range(0, SCAN_CHUNK, 2 * half):
            mid = start + half
            bnd = cum[mid - 1:mid, :]
            k_parts += [k[start:mid] * jnp.exp2(bnd - cum[start:mid]), zeros]
            q_parts += [zeros,
                        q[mid:mid + half] * jnp.exp2(cum[mid:mid + half] - bnd)]
        return jnp.concatenate(q_parts, axis=0), jnp.concatenate(k_parts, axis=0)
    if half == 1:
        return q * jnp.exp2(g), k
    groups = SCAN_CHUNK // VREG_ROWS
    cum3 = cum.reshape(groups, VREG_ROWS, lanes)
    sub = lax.broadcasted_iota(jnp.int32, cum3.shape, 1)
    bnd = None
    for start in range(0, VREG_ROWS, 2 * half):
        piece = jnp.broadcast_to(
            cum3[:, start + half - 1:start + half, :], cum3.shape)
        bnd = piece if bnd is None else jnp.where(sub >= start, piece, bnd)
    e = jnp.exp2(-jnp.abs(cum3 - bnd)).reshape(SCAN_CHUNK, lanes)
    return q * e, k * e


def _stage_head(rows, lanes, q_ref, k_ref, v_ref, g_ref, chunk, stage):
    ql_ref, ku_ref, qin_ref, kout_ref, v16_ref, direct_ref, decay_ref = stage
    q, k, v, g = (ref[rows, lanes] for ref in (q_ref, k_ref, v_ref, g_ref))
    cum = _chunk_cumsum(g)
    last = cum[SCAN_CHUNK - 1:SCAN_CHUNK, :]
    qin_ref[rows, lanes] = (q * jnp.exp2(cum)).astype(BF16)
    kout_ref[rows, lanes] = (k * jnp.exp2(last - cum)).astype(BF16)
    v16_ref[rows, lanes] = v.astype(BF16)
    decay_ref[chunk, :, lanes] = jnp.exp2(last)
    direct_ref[rows, lanes] = jnp.sum(q * k, axis=-1, keepdims=True) * v
    for idx, half in enumerate(SCAN_HALVES):
        q_l, k_u = _split_operands(q, k, cum, g, half)
        ql_ref[idx, rows, lanes] = q_l.astype(BF16)
        ku_ref[idx, rows, lanes] = k_u.astype(BF16)


def _block_diag(a, b):
    zeros = jnp.zeros_like(a)
    return jnp.concatenate([jnp.concatenate([a, zeros], axis=1),
                            jnp.concatenate([zeros, b], axis=1)], axis=0)


def _combine_head_pair(pair, rows, chunk, level_of_pair, stage, state_ref):
    ql_ref, ku_ref, qin_ref, kout_ref, v16_ref, direct_ref, decay_ref = stage
    nt = (((1,), (1,)), ((), ()))
    first, second = 2 * pair, 2 * pair + 1
    lanes_a = slice(first * HG_DIM, second * HG_DIM)
    lanes_b = slice(second * HG_DIM, (second + 1) * HG_DIM)
    lanes = slice(first * HG_DIM, (second + 1) * HG_DIM)

    level_scores = [
        lax.dot_general(
            ql_ref[idx, rows, lanes],
            _block_diag(ku_ref[idx, rows, lanes_a], ku_ref[idx, rows, lanes_b]),
            nt, preferred_element_type=F32)
        for idx in range(len(SCAN_HALVES))]
    state_a, state_b = state_ref[first], state_ref[second]
    out = direct_ref[rows, lanes] + lax.dot_general(
        qin_ref[rows, lanes],
        _block_diag(state_a.astype(BF16), state_b.astype(BF16)), nt,
        preferred_element_type=F32)
    v_bd = _block_diag(v16_ref[rows, lanes_a], v16_ref[rows, lanes_b])

    def advance_state():
        update = lax.dot_general(
            v_bd, _block_diag(kout_ref[rows, lanes_a], kout_ref[rows, lanes_b]),
            (((0,), (0,)), ((), ())), preferred_element_type=F32)
        state_ref[first] = (state_a * decay_ref[chunk, :, lanes_a]
                            + update[:HG_DIM, :HG_DIM])
        state_ref[second] = (state_b * decay_ref[chunk, :, lanes_b]
                             + update[HG_DIM:, HG_DIM:])

    def finish():
        scores = level_scores[0]
        for half, s in zip(SCAN_HALVES[1:], level_scores[1:]):
            scores = jnp.where(level_of_pair[half], s, scores)
        weighted = jnp.dot(scores.astype(BF16), v_bd, preferred_element_type=F32)
        return lambda: out + weighted

    return finish, advance_state


def _hgrn_lower_bound(lb_raw, layer):
    e = jnp.exp(lb_raw - jnp.max(lb_raw, axis=0, keepdims=True))
    total = jnp.sum(e, axis=0, keepdims=True)
    below = jnp.zeros_like(total)
    for m in range(1, layer + 1):
        below = below + e[m:m + 1]
    return below / total


def _hgrn_kernel(x_ref, mod_ref, ng_ref, win_ref, lb_ref, gn_ref, wout_ref,
                 o_ref, h_scr, q_scr, k_scr, v_scr, lf_scr, sg_scr, o_scr,
                 state_scr, ql_scr, ku_scr, qin_scr, kout_scr, v16_scr,
                 decay_scr, *, layer):
    tm, d = x_ref.shape
    stage = (ql_scr, ku_scr, qin_scr, kout_scr, v16_scr, o_scr, decay_scr)

    @pl.when(pl.program_id(1) == 0)
    def _():
        state_scr[...] = jnp.zeros_like(state_scr)

    x = x_ref[...]
    mod = mod_ref[0]
    h = _modulated_rms_norm(x, ng_ref[...], mod[1:2], mod[0:1])
    h_scr[...] = h.astype(BF16)

    def proj(i):
        return jnp.dot(h_scr[...], win_ref[:, i * d:(i + 1) * d],
                       preferred_element_type=F32)

    lb = _hgrn_lower_bound(lb_ref[...], layer)
    q_lin = proj(0)
    f_lin = proj(1)
    q_scr[...] = _silu(q_lin)
    v_scr[...] = proj(2)
    f = lb + (1.0 - lb) * jax.nn.sigmoid(f_lin)
    lf_scr[...] = jnp.log(f) * LOG2_E
    k_scr[...] = 1.0 - f
    sg_scr[...] = _silu(proj(3))

    head_lanes = [slice(hd * HG_DIM, (hd + 1) * HG_DIM)
                  for hd in range(HG_HEADS)]

    def chunk_rows(c):
        return pl.ds(pl.multiple_of(c * SCAN_CHUNK, SCAN_CHUNK), SCAN_CHUNK)

    def normalize_pair(rows, pair, out):
        for hd in (2 * pair, 2 * pair + 1):
            o = out[:, (hd % 2) * HG_DIM:(hd % 2 + 1) * HG_DIM]
            ms = jnp.mean(o * o, axis=-1, keepdims=True)
            o_scr[rows, head_lanes[hd]] = o * lax.rsqrt(ms + EPS)

    def chunk_body(c, carry):
        rows = chunk_rows(c)
        row_s = lax.broadcasted_iota(jnp.int32, (SCAN_CHUNK, 2 * SCAN_CHUNK), 0)
        col_s = lax.broadcasted_iota(
            jnp.int32, (SCAN_CHUNK, 2 * SCAN_CHUNK), 1) & (SCAN_CHUNK - 1)
        level_of_pair = {
            half: (((row_s ^ col_s) >> (half.bit_length() - 1)) == 1)
            & (row_s > col_s) for half in SCAN_HALVES[1:]}
        finish, advance_state, emit = {}, {}, {}
        for step in SCAN_SCHEDULE.split():
            kind, pair = step[0], int(step[1:])
            if kind == "F":
                for hd in (2 * pair, 2 * pair + 1):
                    _stage_head(rows, head_lanes[hd], q_scr, k_scr, v_scr,
                                lf_scr, c, stage)
                finish[pair], advance_state[pair] = _combine_head_pair(
                    pair, rows, c, level_of_pair, stage, state_scr)
            elif kind == "M":
                emit[pair] = finish[pair]()
            elif kind == "B":
                normalize_pair(rows, pair, emit[pair]())
            else:
                advance_state[pair]()
        return carry

    lax.fori_loop(0, tm // SCAN_CHUNK, chunk_body, 0)

    gated = (o_scr[...] * gn_ref[...]) * sg_scr[...]
    y = jnp.dot(gated.astype(BF16), wout_ref[...], preferred_element_type=F32)
    o_ref[...] = x + mod[2:3] * y


def _hgrn_layer(x2, mod, norm_g, w_in, lb_raw, layer, gn_g, w_out, batch, seq):
    n, d = x2.shape
    tm = TOKEN_TILE
    tiles_per_seq = seq // tm
    act = pltpu.VMEM((tm, d), F32)
    return pl.pallas_call(
        functools.partial(_hgrn_kernel, layer=layer),
        grid=(batch, tiles_per_seq),
        in_specs=[
            pl.BlockSpec((tm, d), lambda b, t: (b * tiles_per_seq + t, 0)),
            pl.BlockSpec((1, 6, d), lambda b, t: (b, 0, 0)),
            _resident((1, d)),
            _resident(w_in.shape),
            _resident(lb_raw.shape),
            _resident((1, d)),
            _resident(w_out.shape),
        ],
        out_specs=pl.BlockSpec((tm, d), lambda b, t: (b * tiles_per_seq + t, 0)),
        out_shape=jax.ShapeDtypeStruct((n, d), F32),
        scratch_shapes=[
            pltpu.VMEM((tm, d), BF16),
            act, act, act, act, act, act,
            pltpu.VMEM((HG_HEADS, HG_DIM, HG_DIM), F32),
            pltpu.VMEM((len(SCAN_HALVES), tm, d), BF16),
            pltpu.VMEM((len(SCAN_HALVES), tm, d), BF16),
            pltpu.VMEM((tm, d), BF16),
            pltpu.VMEM((tm, d), BF16),
            pltpu.VMEM((tm, d), BF16),
            pltpu.VMEM((tm // SCAN_CHUNK, 1, d), F32),
        ],
        compiler_params=pltpu.CompilerParams(
            dimension_semantics=("arbitrary", "arbitrary"),
            vmem_limit_bytes=VMEM_LIMIT_BYTES),
        name="hgrn_mixer",
    )(x2, mod, norm_g.reshape(1, d), w_in, lb_raw.astype(F32),
      gn_g.reshape(1, d), w_out)


def _ffn_kernel(x_ref, mod_ref, ng_ref, wup_ref, cw_ref, cb_ref, wdown_ref,
                fg_ref, o_ref, h_scr, p_scr, halo_scr, *, final_norm):
    tm, d = x_ref.shape
    hidden = wdown_ref.shape[0]

    @pl.when(pl.program_id(1) == 0)
    def _():
        halo_scr[...] = jnp.zeros_like(halo_scr)

    x = x_ref[...]
    mod = mod_ref[0]
    h = _modulated_rms_norm(x, ng_ref[...], mod[4:5], mod[3:4])
    h_scr[...] = h.astype(BF16)

    cn = FFN_BLOCK
    row = lax.broadcasted_iota(jnp.int32, (tm, cn), 0)

    def conv(cols):
        a = jnp.dot(h_scr[...], wup_ref[:, cols], preferred_element_type=F32)
        prev1 = halo_scr[1:2, cols]
        prev2 = halo_scr[0:1, cols]
        halo_scr[:, cols] = a[tm - 2:tm, :]
        a1 = jnp.where(row == 0, prev1, pltpu.roll(a, 1, 0))
        a2 = jnp.where(row == 0, prev2,
                       jnp.where(row == 1, prev1, pltpu.roll(a, 2, 0)))
        return (cb_ref[:, cols] + cw_ref[0:1, cols] * a2
                + cw_ref[1:2, cols] * a1 + cw_ref[2:3, cols] * a)

    for j in range(hidden // cn):
        gate = conv(slice(j * cn, (j + 1) * cn))
        val = conv(slice(hidden + j * cn, hidden + (j + 1) * cn))
        p_scr[:, j * cn:(j + 1) * cn] = (_gelu(gate) * val).astype(BF16)

    y = jnp.dot(p_scr[...], wdown_ref[...], preferred_element_type=F32)
    out = x + mod[5:6] * y
    if final_norm:
        ms = jnp.mean(out * out, axis=-1, keepdims=True)
        out = (out * lax.rsqrt(ms + EPS)) * fg_ref[...]
    o_ref[...] = out


def _ffn_layer(x2, mod, norm_g, w_up, conv_w, conv_b, w_down, final_g, batch,
               seq, final_norm):
    n, d = x2.shape
    tm = TOKEN_TILE
    tiles_per_seq = seq // tm
    hidden = w_down.shape[0]
    return pl.pallas_call(
        functools.partial(_ffn_kernel, final_norm=final_norm),
        grid=(batch, tiles_per_seq),
        in_specs=[
            pl.BlockSpec((tm, d), lambda b, t: (b * tiles_per_seq + t, 0)),
            pl.BlockSpec((1, 6, d), lambda b, t: (b, 0, 0)),
            _resident((1, d)),
            _resident(w_up.shape),
            _resident(conv_w.shape),
            _resident((1, 2 * hidden)),
            _resident(w_down.shape),
            _resident((1, d)),
        ],
        out_specs=pl.BlockSpec((tm, d), lambda b, t: (b * tiles_per_seq + t, 0)),
        out_shape=jax.ShapeDtypeStruct((n, d), F32),
        scratch_shapes=[
            pltpu.VMEM((tm, d), BF16),
            pltpu.VMEM((tm, hidden), BF16),
            pltpu.VMEM((CONV_WIDTH - 1, 2 * hidden), F32),
        ],
        compiler_params=pltpu.CompilerParams(
            dimension_semantics=("arbitrary", "arbitrary"),
            vmem_limit_bytes=VMEM_LIMIT_BYTES),
        name="conv_ffn",
    )(x2, mod, norm_g.reshape(1, d), w_up, conv_w, conv_b.reshape(1, 2 * hidden),
      w_down, final_g.reshape(1, d))


def kernel(x, c, gm_w_in, gm_ln_g, gm_ln_b, gm_w_s, gm_b_s, gm_w_out, hg_w_in, hg_lb, hg_gn_g, hg_w_out, ffn_w_up, ffn_conv_w, ffn_conv_b, ffn_w_down, norm_g, ada_w, ada_b, final_g):
    batch, seq, d = x.shape
    depth = ada_w.shape[0]
    assert seq % TOKEN_TILE == 0 and TOKEN_TILE % GM_BLOCK == 0

    c_pad = jnp.pad(c, ((0, 8 - batch), (0, 0)))
    mod_all = _ada_modulation(c_pad, ada_w, ada_b)
    mod_all = mod_all[:, :batch].reshape(depth, batch, 6, d)

    head_dim = gm_w_out.shape[1] // GM_HEADS
    x2 = x.reshape(batch * seq, d)
    for i in range(depth):
        j = i // 2
        if i % 2 == 0:
            b_s_tile = jnp.repeat(gm_b_s[j].T, head_dim, axis=1)
            x2 = _gmlp_layer(x2, mod_all[i], norm_g[i, 0],
                             gm_w_in[j].astype(BF16), gm_ln_g[j], gm_ln_b[j],
                             gm_w_s[j], b_s_tile, gm_w_out[j].astype(BF16), seq)
        else:
            x2 = _hgrn_layer(x2, mod_all[i], norm_g[i, 0],
                             hg_w_in[j].astype(BF16), hg_lb, j, hg_gn_g[j],
                             hg_w_out[j].astype(BF16), batch, seq)
        x2 = _ffn_layer(x2, mod_all[i], norm_g[i, 1], ffn_w_up[i].astype(BF16),
                        ffn_conv_w[i], ffn_conv_b[i],
                        ffn_w_down[i].astype(BF16), final_g, batch, seq,
                        final_norm=(i == depth - 1))
    return x2.reshape(batch, seq, d)
```

```python
import functools

import jax
import jax.numpy as jnp
from jax import lax
from jax.experimental import pallas as pl
from jax.experimental.pallas import tpu as pltpu

F32 = jnp.float32
BF16 = jnp.bfloat16

EPS = 1e-6
CHUNK = 64
GM_BLOCK = 128
GM_HEADS = 8
HG_HEADS = 8
HG_DIM = 128
SCAN_CHUNK = 64
SCAN_HALVES = (32, 16, 8, 4, 2, 1)
SCAN_SCHEDULE = "F0 F1 F2 F3 M0 M1 B0 U0 M2 B1 U1 M3 B2 U2 U3 B3"
VREG_ROWS = 8
BF16_TILE_ROWS = 16
LOG2_E = 1.4426950408889634
CONV_WIDTH = 3
FFN_BLOCK = 256
SQRT_HALF = 0.7071067811865476

TOKEN_TILE = 512
VMEM_LIMIT_BYTES = 56 * 1024 * 1024


def _gelu(x):
    return 0.5 * x * (1.0 + lax.erf(x * SQRT_HALF))


def _silu(x):
    return x * (0.5 * jnp.tanh(0.5 * x) + 0.5)


def _modulated_rms_norm(x, gain, scale, shift):
    ms = jnp.mean(x * x, axis=-1, keepdims=True)
    return (x * lax.rsqrt(ms + EPS)) * (gain * (1.0 + scale)) + shift


def _resident(shape):
    zeros = (0,) * len(shape)
    return pl.BlockSpec(shape, lambda *_: zeros, pipeline_mode=pl.Buffered(1))


def _resident_layer(stacked_shape, layer):
    index = (layer,) + (0,) * (len(stacked_shape) - 1)
    return pl.BlockSpec((None,) + tuple(stacked_shape[1:]), lambda *_: index,
                        pipeline_mode=pl.Buffered(1))


def _cast_specs(stacked, layer, grid_steps, step_of):
    _, rows, cols = stacked.shape
    block = next(r for r in range(BF16_TILE_ROWS, rows + 1, BF16_TILE_ROWS)
                 if rows % r == 0 and r * grid_steps >= rows)
    last = rows // block - 1

    def block_index(*grid_indices):
        return jnp.minimum(step_of(*grid_indices), last)

    return (pl.BlockSpec((None, block, cols),
                         lambda *g: (layer, block_index(*g), 0)),
            pl.BlockSpec((block, cols), lambda *g: (block_index(*g), 0)),
            jax.ShapeDtypeStruct((rows, cols), BF16))


def _with_casts(body, n_inputs, n_casts):
    def kernel(*refs):
        out_at = n_inputs + n_casts
        for src, dst in zip(refs[n_inputs:out_at],
                            refs[out_at + 1:out_at + 1 + n_casts]):
            dst[...] = src[...].astype(BF16)
        body(*refs[:n_inputs], refs[out_at], *refs[out_at + 1 + n_casts:])
    return kernel


def _call_with_casts(body, n_inputs, next_weights, grid, step_of, in_specs,
                     out_spec, out_shape, inputs, **kwargs):
    steps = 1
    for extent in grid:
        steps *= extent
    casts = [_cast_specs(w, layer, steps, step_of) for w, layer in next_weights]
    return pl.pallas_call(
        _with_casts(body, n_inputs, len(casts)),
        grid=grid,
        in_specs=list(in_specs) + [c[0] for c in casts],
        out_specs=[out_spec] + [c[1] for c in casts],
        out_shape=[out_shape] + [c[2] for c in casts],
        **kwargs,
    )(*inputs, *[w for w, _ in next_weights])


def _ada_kernel(c_ref, w_ref, b_ref, o_ref):
    cond = _silu(c_ref[...])
    o_ref[0] = jnp.dot(cond, w_ref[0], preferred_element_type=F32) + b_ref[0]


def _ada_modulation(c_pad, ada_w, ada_b, next_weights):
    depth, d, n = ada_w.shape
    rows = c_pad.shape[0]
    tn = 1536
    return _call_with_casts(
        _ada_kernel, 3, next_weights,
        grid=(depth, n // tn),
        step_of=lambda l, j: l * (n // tn) + j,
        in_specs=[
            pl.BlockSpec((rows, d), lambda l, j: (0, 0)),
            pl.BlockSpec((1, d, tn), lambda l, j: (l, 0, j)),
            pl.BlockSpec((1, 1, tn), lambda l, j: (l, 0, j)),
        ],
        out_spec=pl.BlockSpec((1, rows, tn), lambda l, j: (l, 0, j)),
        out_shape=jax.ShapeDtypeStruct((depth, rows, n), F32),
        inputs=(c_pad, ada_w, ada_b.reshape(depth, 1, n)),
        compiler_params=pltpu.CompilerParams(
            dimension_semantics=("arbitrary", "arbitrary"),
            vmem_limit_bytes=VMEM_LIMIT_BYTES),
        name="ada_modulation",
    )


def _gmlp_kernel(x_ref, mod_ref, ng_ref, win_ref, lng_ref, lnb_ref, ws_ref,
                 bs_ref, wout_ref, o_ref, h_scr, u_scr, v_scr, vn_scr, gt_scr):
    tm, d = x_ref.shape
    width = u_scr.shape[1]
    head_dim = width // GM_HEADS
    x = x_ref[...]
    mod = mod_ref[0]
    h = _modulated_rms_norm(x, ng_ref[...], mod[1:2], mod[0:1])
    h_scr[...] = h.astype(BF16)

    cn = 512

    def store_gelu(dst_ref, j, z):
        dst_ref[:, j * cn:(j + 1) * cn] = _gelu(z)

    def layer_norm_v():
        v = v_scr[...]
        mu = jnp.mean(v, axis=-1, keepdims=True)
        dv = v - mu
        var = jnp.mean(dv * dv, axis=-1, keepdims=True)
        vn = dv * lax.rsqrt(var + EPS) * lng_ref[...] + lnb_ref[...]
        vn_scr[...] = vn.astype(BF16)

    row = lax.broadcasted_iota(jnp.int32, (GM_BLOCK, GM_BLOCK), 0)
    col = lax.broadcasted_iota(jnp.int32, (GM_BLOCK, GM_BLOCK), 1)
    allowed = (row >= CHUNK) | (col < CHUNK)

    def gate(hd):
        w = jnp.where(allowed, ws_ref[hd], 0.0).astype(BF16)
        cols = slice(hd * head_dim, (hd + 1) * head_dim)
        for r in range(tm // GM_BLOCK):
            rows = slice(r * GM_BLOCK, (r + 1) * GM_BLOCK)
            s = jnp.dot(w, vn_scr[rows, cols], preferred_element_type=F32)
            s = s + bs_ref[:, cols]
            gt_scr[rows, cols] = (u_scr[rows, cols] * s).astype(BF16)

    blocks = width // cn
    heads_per_block = cn // head_dim
    elementwise = []
    for dst_ref, first_col in ((v_scr, width), (u_scr, 0)):
        for j in range(blocks):
            cols = slice(first_col + j * cn, first_col + (j + 1) * cn)
            z = jnp.dot(h_scr[...], win_ref[:, cols], preferred_element_type=F32)
            for work in elementwise:
                work()
            elementwise = [functools.partial(store_gelu, dst_ref, j, z)]
            if dst_ref is v_scr and j == blocks - 1:
                elementwise.append(layer_norm_v)
            if dst_ref is u_scr:
                elementwise += [functools.partial(gate, hd) for hd in range(
                    j * heads_per_block, (j + 1) * heads_per_block)]
    for work in elementwise:
        work()
    y = jnp.dot(gt_scr[...], wout_ref[...], preferred_element_type=F32)
    o_ref[...] = x + mod[2:3] * y


def _gmlp_layer(x2, mod, norm_g, w_in, ln_g, ln_b, w_s, layer, b_s_tile, w_out,
                seq, next_weights):
    n, d = x2.shape
    tm = TOKEN_TILE
    width = w_out.shape[0]
    tiles_per_seq = seq // tm
    return _call_with_casts(
        _gmlp_kernel, 9, next_weights,
        grid=(n // tm,),
        step_of=lambda t: t,
        in_specs=[
            pl.BlockSpec((tm, d), lambda t: (t, 0)),
            pl.BlockSpec((1, 6, d), lambda t: (t // tiles_per_seq, 0, 0)),
            _resident((1, d)),
            _resident(w_in.shape),
            _resident((1, width)),
            _resident((1, width)),
            _resident_layer(w_s.shape, layer),
            _resident(b_s_tile.shape),
            _resident(w_out.shape),
        ],
        out_spec=pl.BlockSpec((tm, d), lambda t: (t, 0)),
        out_shape=jax.ShapeDtypeStruct((n, d), F32),
        inputs=(x2, mod, norm_g.reshape(1, d), w_in, ln_g.reshape(1, width),
                ln_b.reshape(1, width), w_s, b_s_tile, w_out),
        scratch_shapes=[
            pltpu.VMEM((tm, d), BF16),
            pltpu.VMEM((tm, width), F32),
            pltpu.VMEM((tm, width), F32),
            pltpu.VMEM((tm, width), BF16),
            pltpu.VMEM((tm, width), BF16),
        ],
        compiler_params=pltpu.CompilerParams(
            dimension_semantics=("arbitrary",),
            vmem_limit_bytes=VMEM_LIMIT_BYTES),
        name="gmlp_mixer",
    )


def _chunk_cumsum(g):
    groups = SCAN_CHUNK // VREG_ROWS
    g3 = g.reshape(groups, VREG_ROWS, g.shape[1])
    sub = lax.broadcasted_iota(jnp.int32, g3.shape, 1)
    step = 1
    while step < VREG_ROWS:
        g3 = g3 + jnp.where(sub >= step, pltpu.roll(g3, step, 1), 0.0)
        step *= 2
    pieces = [g3[0]]
    for r in range(1, groups):
        pieces.append(g3[r] + pieces[-1][VREG_ROWS - 1:VREG_ROWS, :])
    return jnp.concatenate(pieces, axis=0)


def _split_operands(q, k, cum, g, half):
    lanes = q.shape[1]
    if half >= VREG_ROWS:
        zeros = jnp.zeros((half, lanes), F32)
        q_parts, k_parts = [], []
        for start in range(0, SCAN_CHUNK, 2 * half):
            mid = start + half
            bnd = cum[mid - 1:mid, :]
            k_parts += [k[start:mid] * jnp.exp2(bnd - cum[start:mid]), zeros]
            q_parts += [zeros,
                        q[mid:mid + half] * jnp.exp2(cum[mid:mid + half] - bnd)]
        return jnp.concatenate(q_parts, axis=0), jnp.concatenate(k_parts, axis=0)
    if half == 1:
        return q * jnp.exp2(g), k
    groups = SCAN_CHUNK // VREG_ROWS
    cum3 = cum.reshape(groups, VREG_ROWS, lanes)
    sub = lax.broadcasted_iota(jnp.int32, cum3.shape, 1)
    bnd = None
    for start in range(0, VREG_ROWS, 2 * half):
        piece = jnp.broadcast_to(
            cum3[:, start + half - 1:start + half, :], cum3.shape)
        bnd = piece if bnd is None else jnp.where(sub >= start, piece, bnd)
    e = jnp.exp2(-jnp.abs(cum3 - bnd)).reshape(SCAN_CHUNK, lanes)
    return q * e, k * e


def _stage_head(rows, lanes, q_ref, k_ref, v_ref, g_ref, chunk, stage):
    ql_ref, ku_ref, qin_ref, kout_ref, v16_ref, direct_ref, decay_ref = stage
    q, k, v, g = (ref[rows, lanes] for ref in (q_ref, k_ref, v_ref, g_ref))
    cum = _chunk_cumsum(g)
    last = cum[SCAN_CHUNK - 1:SCAN_CHUNK, :]
    qin_ref[rows, lanes] = (q * jnp.exp2(cum)).astype(BF16)
    kout_ref[rows, lanes] = (k * jnp.exp2(last - cum)).astype(BF16)
    v16_ref[rows, lanes] = v.astype(BF16)
    decay_ref[chunk, :, lanes] = jnp.exp2(last)
    direct_ref[rows, lanes] = jnp.sum(q * k, axis=-1, keepdims=True) * v
    for idx, half in enumerate(SCAN_HALVES):
        q_l, k_u = _split_operands(q, k, cum, g, half)
        ql_ref[idx, rows, lanes] = q_l.astype(BF16)
        ku_ref[idx, rows, lanes] = k_u.astype(BF16)


def _block_diag(a, b):
    zeros = jnp.zeros_like(a)
    return jnp.concatenate([jnp.concatenate([a, zeros], axis=1),
                            jnp.concatenate([zeros, b], axis=1)], axis=0)


def _combine_head_pair(pair, rows, chunk, level_of_pair, stage, state_ref):
    ql_ref, ku_ref, qin_ref, kout_ref, v16_ref, direct_ref, decay_ref = stage
    nt = (((1,), (1,)), ((), ()))
    first, second = 2 * pair, 2 * pair + 1
    lanes_a = slice(first * HG_DIM, second * HG_DIM)
    lanes_b = slice(second * HG_DIM, (second + 1) * HG_DIM)
    lanes = slice(first * HG_DIM, (second + 1) * HG_DIM)

    level_scores = [
        lax.dot_general(
            ql_ref[idx, rows, lanes],
            _block_diag(ku_ref[idx, rows, lanes_a], ku_ref[idx, rows, lanes_b]),
            nt, preferred_element_type=F32)
        for idx in range(len(SCAN_HALVES))]
    state_a, state_b = state_ref[first], state_ref[second]
    out = direct_ref[rows, lanes] + lax.dot_general(
        qin_ref[rows, lanes],
        _block_diag(state_a.astype(BF16), state_b.astype(BF16)), nt,
        preferred_element_type=F32)
    v_bd = _block_diag(v16_ref[rows, lanes_a], v16_ref[rows, lanes_b])

    def advance_state():
        update = lax.dot_general(
            v_bd, _block_diag(kout_ref[rows, lanes_a], kout_ref[rows, lanes_b]),
            (((0,), (0,)), ((), ())), preferred_element_type=F32)
        state_ref[first] = (state_a * decay_ref[chunk, :, lanes_a]
                            + update[:HG_DIM, :HG_DIM])
        state_ref[second] = (state_b * decay_ref[chunk, :, lanes_b]
                             + update[HG_DIM:, HG_DIM:])

    def finish():
        scores = level_scores[0]
        for half, s in zip(SCAN_HALVES[1:], level_scores[1:]):
            scores = jnp.where(level_of_pair[half], s, scores)
        weighted = jnp.dot(scores.astype(BF16), v_bd, preferred_element_type=F32)
        return lambda: out + weighted

    return finish, advance_state


def _hgrn_lower_bound(lb_raw, layer):
    e = jnp.exp(lb_raw - jnp.max(lb_raw, axis=0, keepdims=True))
    total = jnp.sum(e, axis=0, keepdims=True)
    below = jnp.zeros_like(total)
    for m in range(1, layer + 1):
        below = below + e[m:m + 1]
    return below / total


def _hgrn_kernel(x_ref, mod_ref, ng_ref, win_ref, lb_ref, gn_ref, wout_ref,
                 o_ref, h_scr, q_scr, k_scr, v_scr, lf_scr, sg_scr, o_scr,
                 state_scr, ql_scr, ku_scr, qin_scr, kout_scr, v16_scr,
                 decay_scr, *, layer):
    tm, d = x_ref.shape
    stage = (ql_scr, ku_scr, qin_scr, kout_scr, v16_scr, o_scr, decay_scr)

    @pl.when(pl.program_id(1) == 0)
    def _():
        state_scr[...] = jnp.zeros_like(state_scr)

    x = x_ref[...]
    mod = mod_ref[0]
    h = _modulated_rms_norm(x, ng_ref[...], mod[1:2], mod[0:1])
    h_scr[...] = h.astype(BF16)

    def proj(i):
        return jnp.dot(h_scr[...], win_ref[:, i * d:(i + 1) * d],
                       preferred_element_type=F32)

    lb = _hgrn_lower_bound(lb_ref[...], layer)
    q_lin = proj(0)
    f_lin = proj(1)
    q_scr[...] = _silu(q_lin)
    v_scr[...] = proj(2)
    f = lb + (1.0 - lb) * jax.nn.sigmoid(f_lin)
    lf_scr[...] = jnp.log(f) * LOG2_E
    k_scr[...] = 1.0 - f
    sg_scr[...] = _silu(proj(3))

    head_lanes = [slice(hd * HG_DIM, (hd + 1) * HG_DIM)
                  for hd in range(HG_HEADS)]

    def chunk_rows(c):
        return pl.ds(pl.multiple_of(c * SCAN_CHUNK, SCAN_CHUNK), SCAN_CHUNK)

    def normalize_pair(rows, pair, out):
        for hd in (2 * pair, 2 * pair + 1):
            o = out[:, (hd % 2) * HG_DIM:(hd % 2 + 1) * HG_DIM]
            ms = jnp.mean(o * o, axis=-1, keepdims=True)
            o_scr[rows, head_lanes[hd]] = o * lax.rsqrt(ms + EPS)

    def chunk_body(c, carry):
        rows = chunk_rows(c)
        row_s = lax.broadcasted_iota(jnp.int32, (SCAN_CHUNK, 2 * SCAN_CHUNK), 0)
        col_s = lax.broadcasted_iota(
            jnp.int32, (SCAN_CHUNK, 2 * SCAN_CHUNK), 1) & (SCAN_CHUNK - 1)
        level_of_pair = {
            half: (((row_s ^ col_s) >> (half.bit_length() - 1)) == 1)
            & (row_s > col_s) for half in SCAN_HALVES[1:]}
        finish, advance_state, emit = {}, {}, {}
        for step in SCAN_SCHEDULE.split():
            kind, pair = step[0], int(step[1:])
            if kind == "F":
                for hd in (2 * pair, 2 * pair + 1):
                    _stage_head(rows, head_lanes[hd], q_scr, k_scr, v_scr,
                                lf_scr, c, stage)
                finish[pair], advance_state[pair] = _combine_head_pair(
                    pair, rows, c, level_of_pair, stage, state_scr)
            elif kind == "M":
                emit[pair] = finish[pair]()
            elif kind == "B":
                normalize_pair(rows, pair, emit[pair]())
            else:
                advance_state[pair]()
        return carry

    lax.fori_loop(0, tm // SCAN_CHUNK, chunk_body, 0)

    gated = (o_scr[...] * gn_ref[...]) * sg_scr[...]
    y = jnp.dot(gated.astype(BF16), wout_ref[...], preferred_element_type=F32)
    o_ref[...] = x + mod[2:3] * y


def _hgrn_layer(x2, mod, norm_g, w_in, lb_raw, layer, gn_g, w_out, batch, seq,
                next_weights):
    n, d = x2.shape
    tm = TOKEN_TILE
    tiles_per_seq = seq // tm
    act = pltpu.VMEM((tm, d), F32)
    return _call_with_casts(
        functools.partial(_hgrn_kernel, layer=layer), 7, next_weights,
        grid=(batch, tiles_per_seq),
        step_of=lambda b, t: b * tiles_per_seq + t,
        in_specs=[
            pl.BlockSpec((tm, d), lambda b, t: (b * tiles_per_seq + t, 0)),
            pl.BlockSpec((1, 6, d), lambda b, t: (b, 0, 0)),
            _resident((1, d)),
            _resident(w_in.shape),
            _resident(lb_raw.shape),
            _resident((1, d)),
            _resident(w_out.shape),
        ],
        out_spec=pl.BlockSpec((tm, d), lambda b, t: (b * tiles_per_seq + t, 0)),
        out_shape=jax.ShapeDtypeStruct((n, d), F32),
        inputs=(x2, mod, norm_g.reshape(1, d), w_in, lb_raw.astype(F32),
                gn_g.reshape(1, d), w_out),
        scratch_shapes=[
            pltpu.VMEM((tm, d), BF16),
            act, act, act, act, act, act,
            pltpu.VMEM((HG_HEADS, HG_DIM, HG_DIM), F32),
            pltpu.VMEM((len(SCAN_HALVES), tm, d), BF16),
            pltpu.VMEM((len(SCAN_HALVES), tm, d), BF16),
            pltpu.VMEM((tm, d), BF16),
            pltpu.VMEM((tm, d), BF16),
            pltpu.VMEM((tm, d), BF16),
            pltpu.VMEM((tm // SCAN_CHUNK, 1, d), F32),
        ],
        compiler_params=pltpu.CompilerParams(
            dimension_semantics=("arbitrary", "arbitrary"),
            vmem_limit_bytes=VMEM_LIMIT_BYTES),
        name="hgrn_mixer",
    )


def _ffn_kernel(x_ref, mod_ref, ng_ref, wup_ref, cw_ref, cb_ref, wdown_ref,
                fg_ref, o_ref, h_scr, p_scr, halo_scr, *, final_norm):
    tm, d = x_ref.shape
    hidden = wdown_ref.shape[0]

    @pl.when(pl.program_id(1) == 0)
    def _():
        halo_scr[...] = jnp.zeros_like(halo_scr)

    x = x_ref[...]
    mod = mod_ref[0]
    h = _modulated_rms_norm(x, ng_ref[...], mod[4:5], mod[3:4])
    h_scr[...] = h.astype(BF16)

    cn = FFN_BLOCK
    row = lax.broadcasted_iota(jnp.int32, (tm, cn), 0)

    def conv(cols):
        a = jnp.dot(h_scr[...], wup_ref[:, cols], preferred_element_type=F32)
        prev1 = halo_scr[1:2, cols]
        prev2 = halo_scr[0:1, cols]
        halo_scr[:, cols] = a[tm - 2:tm, :]
        a1 = jnp.where(row == 0, prev1, pltpu.roll(a, 1, 0))
        a2 = jnp.where(row == 0, prev2,
                       jnp.where(row == 1, prev1, pltpu.roll(a, 2, 0)))
        return (cb_ref[:, cols] + cw_ref[0:1, cols] * a2
                + cw_ref[1:2, cols] * a1 + cw_ref[2:3, cols] * a)

    for j in range(hidden // cn):
        gate = conv(slice(j * cn, (j + 1) * cn))
        val = conv(slice(hidden + j * cn, hidden + (j + 1) * cn))
        p_scr[:, j * cn:(j + 1) * cn] = (_gelu(gate) * val).astype(BF16)

    y = jnp.dot(p_scr[...], wdown_ref[...], preferred_element_type=F32)
    out = x + mod[5:6] * y
    if final_norm:
        ms = jnp.mean(out * out, axis=-1, keepdims=True)
        out = (out * lax.rsqrt(ms + EPS)) * fg_ref[...]
    o_ref[...] = out


def _ffn_layer(x2, mod, norm_g, w_up, conv_w, layer, conv_b, w_down, final_g,
               batch, seq, final_norm, next_weights):
    n, d = x2.shape
    tm = TOKEN_TILE
    tiles_per_seq = seq // tm
    hidden = w_down.shape[0]
    return _call_with_casts(
        functools.partial(_ffn_kernel, final_norm=final_norm), 8, next_weights,
        grid=(batch, tiles_per_seq),
        step_of=lambda b, t: b * tiles_per_seq + t,
        in_specs=[
            pl.BlockSpec((tm, d), lambda b, t: (b * tiles_per_seq + t, 0)),
            pl.BlockSpec((1, 6, d), lambda b, t: (b, 0, 0)),
            _resident((1, d)),
            _resident(w_up.shape),
            _resident_layer(conv_w.shape, layer),
            _resident((1, 2 * hidden)),
            _resident(w_down.shape),
            _resident((1, d)),
        ],
        out_spec=pl.BlockSpec((tm, d), lambda b, t: (b * tiles_per_seq + t, 0)),
        out_shape=jax.ShapeDtypeStruct((n, d), F32),
        inputs=(x2, mod, norm_g.reshape(1, d), w_up, conv_w,
                conv_b.reshape(1, 2 * hidden), w_down, final_g.reshape(1, d)),
        scratch_shapes=[
            pltpu.VMEM((tm, d), BF16),
            pltpu.VMEM((tm, hidden), BF16),
            pltpu.VMEM((CONV_WIDTH - 1, 2 * hidden), F32),
        ],
        compiler_params=pltpu.CompilerParams(
            dimension_semantics=("arbitrary", "arbitrary"),
            vmem_limit_bytes=VMEM_LIMIT_BYTES),
        name="conv_ffn",
    )


def kernel(x, c, gm_w_in, gm_ln_g, gm_ln_b, gm_w_s, gm_b_s, gm_w_out, hg_w_in, hg_lb, hg_gn_g, hg_w_out, ffn_w_up, ffn_conv_w, ffn_conv_b, ffn_w_down, norm_g, ada_w, ada_b, final_g):
    batch, seq, d = x.shape
    depth = ada_w.shape[0]
    assert seq % TOKEN_TILE == 0 and TOKEN_TILE % GM_BLOCK == 0

    def mixer_weights(i):
        if i % 2 == 0:
            return ((gm_w_in, i // 2), (gm_w_out, i // 2))
        return ((hg_w_in, i // 2), (hg_w_out, i // 2))

    c_pad = jnp.pad(c, ((0, 8 - batch), (0, 0)))
    mod_all, w_a, w_b = _ada_modulation(c_pad, ada_w, ada_b, mixer_weights(0))
    mod_all = mod_all[:, :batch].reshape(depth, batch, 6, d)

    head_dim = gm_w_out.shape[1] // GM_HEADS
    x2 = x.reshape(batch * seq, d)
    for i in range(depth):
        j = i // 2
        ffn_weights = ((ffn_w_up, i), (ffn_w_down, i))
        if i % 2 == 0:
            b_s_tile = jnp.repeat(gm_b_s[j].T, head_dim, axis=1)
            x2, w_a, w_b = _gmlp_layer(
                x2, mod_all[i], norm_g[i, 0], w_a, gm_ln_g[j], gm_ln_b[j], gm_w_s,
                j, b_s_tile, w_b, seq, ffn_weights)
        else:
            x2, w_a, w_b = _hgrn_layer(
                x2, mod_all[i], norm_g[i, 0], w_a, hg_lb, j, hg_gn_g[j], w_b,
                batch, seq, ffn_weights)
        last = i == depth - 1
        x2, *next_mixer = _ffn_layer(
            x2, mod_all[i], norm_g[i, 1], w_a, ffn_conv_w, i, ffn_conv_b[i], w_b,
            final_g, batch, seq, final_norm=last,
            next_weights=() if last else mixer_weights(i + 1))
        if not last:
            w_a, w_b = next_mixer
    return x2.reshape(batch, seq, d)
```

```python
import functools

import jax
import jax.numpy as jnp
from jax import lax
from jax.experimental import pallas as pl
from jax.experimental.pallas import tpu as pltpu

F32 = jnp.float32
BF16 = jnp.bfloat16

EPS = 1e-6
CHUNK = 64
GM_BLOCK = 128
GM_HEADS = 8
HG_HEADS = 8
HG_DIM = 128
SCAN_CHUNK = 64
SCAN_HALVES = (32, 16, 8, 4, 2, 1)
CHUNKS_PER_GROUP = 1
SCAN_SCHEDULE = ("F0a S1a F1a S2a F2a S3a F3a M0a M1a B0a U0a M2a B1a U1a M3a B2a "
                 "U2a U3a B3a N0")
VREG_ROWS = 8
BF16_TILE_ROWS = 16
LOG2_E = 1.4426950408889634
CONV_WIDTH = 3
FFN_BLOCK = 256
SQRT_HALF = 0.7071067811865476

TOKEN_TILE = 512
VMEM_LIMIT_BYTES = 56 * 1024 * 1024


def _gelu(x):
    return 0.5 * x * (1.0 + lax.erf(x * SQRT_HALF))


def _silu(x):
    return x * (0.5 * jnp.tanh(0.5 * x) + 0.5)


def _modulated_rms_norm(x, gain, scale, shift):
    ms = jnp.mean(x * x, axis=-1, keepdims=True)
    return (x * lax.rsqrt(ms + EPS)) * (gain * (1.0 + scale)) + shift


def _resident(shape):
    zeros = (0,) * len(shape)
    return pl.BlockSpec(shape, lambda *_: zeros, pipeline_mode=pl.Buffered(1))


def _resident_layer(stacked_shape, layer):
    index = (layer,) + (0,) * (len(stacked_shape) - 1)
    return pl.BlockSpec((None,) + tuple(stacked_shape[1:]), lambda *_: index,
                        pipeline_mode=pl.Buffered(1))


def _cast_specs(stacked, layer, grid_steps, step_of):
    _, rows, cols = stacked.shape
    block = next(r for r in range(BF16_TILE_ROWS, rows + 1, BF16_TILE_ROWS)
                 if rows % r == 0 and r * grid_steps >= rows)
    last = rows // block - 1

    def block_index(*grid_indices):
        return jnp.minimum(step_of(*grid_indices), last)

    return (pl.BlockSpec((None, block, cols),
                         lambda *g: (layer, block_index(*g), 0)),
            pl.BlockSpec((block, cols), lambda *g: (block_index(*g), 0)),
            jax.ShapeDtypeStruct((rows, cols), BF16))


def _with_casts(body, n_inputs, n_casts):
    def kernel(*refs):
        out_at = n_inputs + n_casts
        for src, dst in zip(refs[n_inputs:out_at],
                            refs[out_at + 1:out_at + 1 + n_casts]):
            dst[...] = src[...].astype(BF16)
        body(*refs[:n_inputs], refs[out_at], *refs[out_at + 1 + n_casts:])
    return kernel


def _call_with_casts(body, n_inputs, next_weights, grid, step_of, in_specs,
                     out_spec, out_shape, inputs, **kwargs):
    steps = 1
    for extent in grid:
        steps *= extent
    casts = [_cast_specs(w, layer, steps, step_of) for w, layer in next_weights]
    return pl.pallas_call(
        _with_casts(body, n_inputs, len(casts)),
        grid=grid,
        in_specs=list(in_specs) + [c[0] for c in casts],
        out_specs=[out_spec] + [c[1] for c in casts],
        out_shape=[out_shape] + [c[2] for c in casts],
        **kwargs,
    )(*inputs, *[w for w, _ in next_weights])


def _ada_kernel(c_ref, w_ref, b_ref, o_ref):
    cond = _silu(c_ref[...])
    o_ref[0] = jnp.dot(cond, w_ref[0], preferred_element_type=F32) + b_ref[0]


def _ada_modulation(c_pad, ada_w, ada_b, next_weights):
    depth, d, n = ada_w.shape
    rows = c_pad.shape[0]
    tn = 1536
    return _call_with_casts(
        _ada_kernel, 3, next_weights,
        grid=(depth, n // tn),
        step_of=lambda l, j: l * (n // tn) + j,
        in_specs=[
            pl.BlockSpec((rows, d), lambda l, j: (0, 0)),
            pl.BlockSpec((1, d, tn), lambda l, j: (l, 0, j)),
            pl.BlockSpec((1, 1, tn), lambda l, j: (l, 0, j)),
        ],
        out_spec=pl.BlockSpec((1, rows, tn), lambda l, j: (l, 0, j)),
        out_shape=jax.ShapeDtypeStruct((depth, rows, n), F32),
        inputs=(c_pad, ada_w, ada_b.reshape(depth, 1, n)),
        compiler_params=pltpu.CompilerParams(
            dimension_semantics=("arbitrary", "arbitrary"),
            vmem_limit_bytes=VMEM_LIMIT_BYTES),
        name="ada_modulation",
    )


def _gmlp_kernel(x_ref, mod_ref, ng_ref, win_ref, lng_ref, lnb_ref, ws_ref,
                 bs_ref, wout_ref, o_ref, h_scr, u_scr, v_scr, vn_scr, gt_scr):
    tm, d = x_ref.shape
    width = u_scr.shape[1]
    head_dim = width // GM_HEADS
    x = x_ref[...]
    mod = mod_ref[0]
    h = _modulated_rms_norm(x, ng_ref[...], mod[1:2], mod[0:1])
    h_scr[...] = h.astype(BF16)

    cn = 512

    def store_gelu(dst_ref, j, z):
        dst_ref[:, j * cn:(j + 1) * cn] = _gelu(z)

    def layer_norm_v():
        v = v_scr[...]
        mu = jnp.mean(v, axis=-1, keepdims=True)
        dv = v - mu
        var = jnp.mean(dv * dv, axis=-1, keepdims=True)
        vn = dv * lax.rsqrt(var + EPS) * lng_ref[...] + lnb_ref[...]
        vn_scr[...] = vn.astype(BF16)

    row = lax.broadcasted_iota(jnp.int32, (GM_BLOCK, GM_BLOCK), 0)
    col = lax.broadcasted_iota(jnp.int32, (GM_BLOCK, GM_BLOCK), 1)
    allowed = (row >= CHUNK) | (col < CHUNK)

    def gate(hd):
        w = jnp.where(allowed, ws_ref[hd], 0.0).astype(BF16)
        cols = slice(hd * head_dim, (hd + 1) * head_dim)
        for r in range(tm // GM_BLOCK):
            rows = slice(r * GM_BLOCK, (r + 1) * GM_BLOCK)
            s = jnp.dot(w, vn_scr[rows, cols], preferred_element_type=F32)
            s = s + bs_ref[:, cols]
            gt_scr[rows, cols] = (u_scr[rows, cols] * s).astype(BF16)

    blocks = width // cn
    heads_per_block = cn // head_dim
    elementwise = []
    for dst_ref, first_col in ((v_scr, width), (u_scr, 0)):
        for j in range(blocks):
            cols = slice(first_col + j * cn, first_col + (j + 1) * cn)
            z = jnp.dot(h_scr[...], win_ref[:, cols], preferred_element_type=F32)
            for work in elementwise:
                work()
            elementwise = [functools.partial(store_gelu, dst_ref, j, z)]
            if dst_ref is v_scr and j == blocks - 1:
                elementwise.append(layer_norm_v)
            if dst_ref is u_scr:
                elementwise += [functools.partial(gate, hd) for hd in range(
                    j * heads_per_block, (j + 1) * heads_per_block)]
    for work in elementwise:
        work()
    y = jnp.dot(gt_scr[...], wout_ref[...], preferred_element_type=F32)
    o_ref[...] = x + mod[2:3] * y


def _gmlp_layer(x2, mod, norm_g, w_in, ln_g, ln_b, w_s, layer, b_s_tile, w_out,
                seq, next_weights):
    n, d = x2.shape
    tm = TOKEN_TILE
    width = w_out.shape[0]
    tiles_per_seq = seq // tm
    return _call_with_casts(
        _gmlp_kernel, 9, next_weights,
        grid=(n // tm,),
        step_of=lambda t: t,
        in_specs=[
            pl.BlockSpec((tm, d), lambda t: (t, 0)),
            pl.BlockSpec((1, 6, d), lambda t: (t // tiles_per_seq, 0, 0)),
            _resident((1, d)),
            _resident(w_in.shape),
            _resident((1, width)),
            _resident((1, width)),
            _resident_layer(w_s.shape, layer),
            _resident(b_s_tile.shape),
            _resident(w_out.shape),
        ],
        out_spec=pl.BlockSpec((tm, d), lambda t: (t, 0)),
        out_shape=jax.ShapeDtypeStruct((n, d), F32),
        inputs=(x2, mod, norm_g.reshape(1, d), w_in, ln_g.reshape(1, width),
                ln_b.reshape(1, width), w_s, b_s_tile, w_out),
        scratch_shapes=[
            pltpu.VMEM((tm, d), BF16),
            pltpu.VMEM((tm, width), F32),
            pltpu.VMEM((tm, width), F32),
            pltpu.VMEM((tm, width), BF16),
            pltpu.VMEM((tm, width), BF16),
        ],
        compiler_params=pltpu.CompilerParams(
            dimension_semantics=("arbitrary",),
            vmem_limit_bytes=VMEM_LIMIT_BYTES),
        name="gmlp_mixer",
    )


def _chunk_cumsum(g):
    groups = SCAN_CHUNK // VREG_ROWS
    g3 = g.reshape(groups, VREG_ROWS, g.shape[1])
    sub = lax.broadcasted_iota(jnp.int32, g3.shape, 1)
    step = 1
    while step < VREG_ROWS:
        g3 = g3 + jnp.where(sub >= step, pltpu.roll(g3, step, 1), 0.0)
        step *= 2
    pieces = [g3[0]]
    for r in range(1, groups):
        pieces.append(g3[r] + pieces[-1][VREG_ROWS - 1:VREG_ROWS, :])
    return jnp.concatenate(pieces, axis=0)


def _split_operands(q, k, cum, g, half):
    lanes = q.shape[1]
    if half >= VREG_ROWS:
        zeros = jnp.zeros((half, lanes), F32)
        q_parts, k_parts = [], []
        for start in range(0, SCAN_CHUNK, 2 * half):
            mid = start + half
            bnd = cum[mid - 1:mid, :]
            k_parts += [k[start:mid] * jnp.exp2(bnd - cum[start:mid]), zeros]
            q_parts += [zeros,
                        q[mid:mid + half] * jnp.exp2(cum[mid:mid + half] - bnd)]
        return jnp.concatenate(q_parts, axis=0), jnp.concatenate(k_parts, axis=0)
    if half == 1:
        return q * jnp.exp2(g), k
    groups = SCAN_CHUNK // VREG_ROWS
    cum3 = cum.reshape(groups, VREG_ROWS, lanes)
    sub = lax.broadcasted_iota(jnp.int32, cum3.shape, 1)
    bnd = None
    for start in range(0, VREG_ROWS, 2 * half):
        piece = jnp.broadcast_to(
            cum3[:, start + half - 1:start + half, :], cum3.shape)
        bnd = piece if bnd is None else jnp.where(sub >= start, piece, bnd)
    e = jnp.exp2(-jnp.abs(cum3 - bnd)).reshape(SCAN_CHUNK, lanes)
    return q * e, k * e


def _stage_head(rows, lanes, q_ref, k_ref, v_ref, g_ref, chunk, stage):
    ql_ref, ku_ref, qin_ref, kout_ref, v16_ref, direct_ref, decay_ref = stage
    q, k, v, g = (ref[rows, lanes] for ref in (q_ref, k_ref, v_ref, g_ref))
    cum = _chunk_cumsum(g)
    last = cum[SCAN_CHUNK - 1:SCAN_CHUNK, :]
    qin_ref[rows, lanes] = (q * jnp.exp2(cum)).astype(BF16)
    kout_ref[rows, lanes] = (k * jnp.exp2(last - cum)).astype(BF16)
    v16_ref[rows, lanes] = v.astype(BF16)
    decay_ref[chunk, :, lanes] = jnp.exp2(last)
    direct_ref[rows, lanes] = jnp.sum(q * k, axis=-1, keepdims=True) * v
    for idx, half in enumerate(SCAN_HALVES):
        q_l, k_u = _split_operands(q, k, cum, g, half)
        ql_ref[idx, rows, lanes] = q_l.astype(BF16)
        ku_ref[idx, rows, lanes] = k_u.astype(BF16)


def _block_diag(a, b):
    zeros = jnp.zeros_like(a)
    return jnp.concatenate([jnp.concatenate([a, zeros], axis=1),
                            jnp.concatenate([zeros, b], axis=1)], axis=0)


def _combine_head_pair(pair, rows, chunk, level_of_pair, stage, state_ref):
    ql_ref, ku_ref, qin_ref, kout_ref, v16_ref, direct_ref, decay_ref = stage
    nt = (((1,), (1,)), ((), ()))
    first, second = 2 * pair, 2 * pair + 1
    lanes_a = slice(first * HG_DIM, second * HG_DIM)
    lanes_b = slice(second * HG_DIM, (second + 1) * HG_DIM)
    lanes = slice(first * HG_DIM, (second + 1) * HG_DIM)

    level_scores = [
        lax.dot_general(
            ql_ref[idx, rows, lanes],
            _block_diag(ku_ref[idx, rows, lanes_a], ku_ref[idx, rows, lanes_b]),
            nt, preferred_element_type=F32)
        for idx in range(len(SCAN_HALVES))]
    state_a, state_b = state_ref[first], state_ref[second]
    out = direct_ref[rows, lanes] + lax.dot_general(
        qin_ref[rows, lanes],
        _block_diag(state_a.astype(BF16), state_b.astype(BF16)), nt,
        preferred_element_type=F32)
    v_bd = _block_diag(v16_ref[rows, lanes_a], v16_ref[rows, lanes_b])

    def advance_state():
        update = lax.dot_general(
            v_bd, _block_diag(kout_ref[rows, lanes_a], kout_ref[rows, lanes_b]),
            (((0,), (0,)), ((), ())), preferred_element_type=F32)
        state_ref[first] = (state_a * decay_ref[chunk, :, lanes_a]
                            + update[:HG_DIM, :HG_DIM])
        state_ref[second] = (state_b * decay_ref[chunk, :, lanes_b]
                             + update[HG_DIM:, HG_DIM:])

    def finish():
        scores = level_scores[0]
        for half, s in zip(SCAN_HALVES[1:], level_scores[1:]):
            scores = jnp.where(level_of_pair[half], s, scores)
        weighted = jnp.dot(scores.astype(BF16), v_bd, preferred_element_type=F32)
        return lambda: out + weighted

    return finish, advance_state


def _hgrn_lower_bound(lb_raw, layer):
    e = jnp.exp(lb_raw - jnp.max(lb_raw, axis=0, keepdims=True))
    total = jnp.sum(e, axis=0, keepdims=True)
    below = jnp.zeros_like(total)
    for m in range(1, layer + 1):
        below = below + e[m:m + 1]
    return below / total


def _hgrn_kernel(x_ref, mod_ref, ng_ref, win_ref, lb_ref, gn_ref, wout_ref,
                 o_ref, h_scr, q_scr, k_scr, v_scr, lf_scr, sg_scr, o_scr,
                 state_scr, ql_scr, ku_scr, qin_scr, kout_scr, v16_scr,
                 decay_scr, *, layer):
    tm, d = x_ref.shape
    stage = (ql_scr, ku_scr, qin_scr, kout_scr, v16_scr, o_scr, decay_scr)

    @pl.when(pl.program_id(1) == 0)
    def _():
        state_scr[...] = jnp.zeros_like(state_scr)

    x = x_ref[...]
    mod = mod_ref[0]
    h = _modulated_rms_norm(x, ng_ref[...], mod[1:2], mod[0:1])
    h_scr[...] = h.astype(BF16)

    def proj(i):
        return jnp.dot(h_scr[...], win_ref[:, i * d:(i + 1) * d],
                       preferred_element_type=F32)

    lb = _hgrn_lower_bound(lb_ref[...], layer)
    q_lin = proj(0)
    f_lin = proj(1)
    q_scr[...] = _silu(q_lin)
    v_scr[...] = proj(2)
    f = lb + (1.0 - lb) * jax.nn.sigmoid(f_lin)
    lf_scr[...] = jnp.log(f) * LOG2_E
    k_scr[...] = 1.0 - f
    sg_scr[...] = _silu(proj(3))

    head_lanes = [slice(hd * HG_DIM, (hd + 1) * HG_DIM)
                  for hd in range(HG_HEADS)]

    def chunk_rows(c):
        start = c * SCAN_CHUNK
        if not isinstance(c, int):
            start = pl.multiple_of(start, SCAN_CHUNK)
        return pl.ds(start, SCAN_CHUNK)

    def normalize_pair(rows, pair, out):
        for hd in (2 * pair, 2 * pair + 1):
            o = out[:, (hd % 2) * HG_DIM:(hd % 2 + 1) * HG_DIM]
            ms = jnp.mean(o * o, axis=-1, keepdims=True)
            o_scr[rows, head_lanes[hd]] = o * lax.rsqrt(ms + EPS)

    def stage_pair(c, pair):
        for hd in (2 * pair, 2 * pair + 1):
            _stage_head(chunk_rows(c), head_lanes[hd], q_scr, k_scr, v_scr,
                        lf_scr, c, stage)

    def chunk_group_body(group, carry, schedule):
        first_chunk = group * CHUNKS_PER_GROUP
        row_s = lax.broadcasted_iota(jnp.int32, (SCAN_CHUNK, 2 * SCAN_CHUNK), 0)
        col_s = lax.broadcasted_iota(
            jnp.int32, (SCAN_CHUNK, 2 * SCAN_CHUNK), 1) & (SCAN_CHUNK - 1)
        level_of_pair = {
            half: (((row_s ^ col_s) >> (half.bit_length() - 1)) == 1)
            & (row_s > col_s) for half in SCAN_HALVES[1:]}
        finish, advance_state, emit = {}, {}, {}
        for step in schedule.split():
            kind, pair = step[0], int(step[1])
            if kind == "N":
                stage_pair(first_chunk + CHUNKS_PER_GROUP, pair)
                continue
            c = first_chunk + "ab".index(step[2])
            item = step[1:]
            if kind == "S":
                stage_pair(c, pair)
            elif kind == "F":
                finish[item], advance_state[item] = _combine_head_pair(
                    pair, chunk_rows(c), c, level_of_pair, stage, state_scr)
            elif kind == "M":
                emit[item] = finish[item]()
            elif kind == "B":
                normalize_pair(chunk_rows(c), pair, emit[item]())
            else:
                advance_state[item]()
        return carry

    groups = tm // (SCAN_CHUNK * CHUNKS_PER_GROUP)
    stage_pair(0, 0)
    lax.fori_loop(0, groups - 1,
                  functools.partial(chunk_group_body, schedule=SCAN_SCHEDULE), 0)
    chunk_group_body(groups - 1, 0, SCAN_SCHEDULE.replace(" N0", ""))

    gated = (o_scr[...] * gn_ref[...]) * sg_scr[...]
    y = jnp.dot(gated.astype(BF16), wout_ref[...], preferred_element_type=F32)
    o_ref[...] = x + mod[2:3] * y


def _hgrn_layer(x2, mod, norm_g, w_in, lb_raw, layer, gn_g, w_out, batch, seq,
                next_weights):
    n, d = x2.shape
    tm = TOKEN_TILE
    tiles_per_seq = seq // tm
    act = pltpu.VMEM((tm, d), F32)
    return _call_with_casts(
        functools.partial(_hgrn_kernel, layer=layer), 7, next_weights,
        grid=(batch, tiles_per_seq),
        step_of=lambda b, t: b * tiles_per_seq + t,
        in_specs=[
            pl.BlockSpec((tm, d), lambda b, t: (b * tiles_per_seq + t, 0)),
            pl.BlockSpec((1, 6, d), lambda b, t: (b, 0, 0)),
            _resident((1, d)),
            _resident(w_in.shape),
            _resident(lb_raw.shape),
            _resident((1, d)),
            _resident(w_out.shape),
        ],
        out_spec=pl.BlockSpec((tm, d), lambda b, t: (b * tiles_per_seq + t, 0)),
        out_shape=jax.ShapeDtypeStruct((n, d), F32),
        inputs=(x2, mod, norm_g.reshape(1, d), w_in, lb_raw.astype(F32),
                gn_g.reshape(1, d), w_out),
        scratch_shapes=[
            pltpu.VMEM((tm, d), BF16),
            act, act, act, act, act, act,
            pltpu.VMEM((HG_HEADS, HG_DIM, HG_DIM), F32),
            pltpu.VMEM((len(SCAN_HALVES), tm, d), BF16),
            pltpu.VMEM((len(SCAN_HALVES), tm, d), BF16),
            pltpu.VMEM((tm, d), BF16),
            pltpu.VMEM((tm, d), BF16),
            pltpu.VMEM((tm, d), BF16),
            pltpu.VMEM((tm // SCAN_CHUNK, 1, d), F32),
        ],
        compiler_params=pltpu.CompilerParams(
            dimension_semantics=("arbitrary", "arbitrary"),
            vmem_limit_bytes=VMEM_LIMIT_BYTES),
        name="hgrn_mixer",
    )


def _ffn_kernel(x_ref, mod_ref, ng_ref, wup_ref, cw_ref, cb_ref, wdown_ref,
                fg_ref, o_ref, h_scr, p_scr, halo_scr, *, final_norm):
    tm, d = x_ref.shape
    hidden = wdown_ref.shape[0]

    @pl.when(pl.program_id(1) == 0)
    def _():
        halo_scr[...] = jnp.zeros_like(halo_scr)

    x = x_ref[...]
    mod = mod_ref[0]
    h = _modulated_rms_norm(x, ng_ref[...], mod[4:5], mod[3:4])
    h_scr[...] = h.astype(BF16)

    cn = FFN_BLOCK
    row = lax.broadcasted_iota(jnp.int32, (tm, cn), 0)

    def conv(cols):
        a = jnp.dot(h_scr[...], wup_ref[:, cols], preferred_element_type=F32)
        prev1 = halo_scr[1:2, cols]
        prev2 = halo_scr[0:1, cols]
        halo_scr[:, cols] = a[tm - 2:tm, :]
        a1 = jnp.where(row == 0, prev1, pltpu.roll(a, 1, 0))
        a2 = jnp.where(row == 0, prev2,
                       jnp.where(row == 1, prev1, pltpu.roll(a, 2, 0)))
        return (SQRT_HALF * cb_ref[:, cols]
                + (SQRT_HALF * cw_ref[0:1, cols]) * a2
                + (SQRT_HALF * cw_ref[1:2, cols]) * a1
                + (SQRT_HALF * cw_ref[2:3, cols]) * a)

    for j in range(hidden // cn):
        gate = conv(slice(j * cn, (j + 1) * cn))
        val = conv(slice(hidden + j * cn, hidden + (j + 1) * cn))
        p_scr[:, j * cn:(j + 1) * cn] = (
            gate * (1.0 + lax.erf(gate)) * val).astype(BF16)

    y = jnp.dot(p_scr[...], wdown_ref[...], preferred_element_type=F32)
    out = x + mod[5:6] * y
    if final_norm:
        ms = jnp.mean(out * out, axis=-1, keepdims=True)
        out = (out * lax.rsqrt(ms + EPS)) * fg_ref[...]
    o_ref[...] = out


def _ffn_layer(x2, mod, norm_g, w_up, conv_w, layer, conv_b, w_down, final_g,
               batch, seq, final_norm, next_weights):
    n, d = x2.shape
    tm = TOKEN_TILE
    tiles_per_seq = seq // tm
    hidden = w_down.shape[0]
    return _call_with_casts(
        functools.partial(_ffn_kernel, final_norm=final_norm), 8, next_weights,
        grid=(batch, tiles_per_seq),
        step_of=lambda b, t: b * tiles_per_seq + t,
        in_specs=[
            pl.BlockSpec((tm, d), lambda b, t: (b * tiles_per_seq + t, 0)),
            pl.BlockSpec((1, 6, d), lambda b, t: (b, 0, 0)),
            _resident((1, d)),
            _resident(w_up.shape),
            _resident_layer(conv_w.shape, layer),
            _resident((1, 2 * hidden)),
            _resident(w_down.shape),
            _resident((1, d)),
        ],
        out_spec=pl.BlockSpec((tm, d), lambda b, t: (b * tiles_per_seq + t, 0)),
        out_shape=jax.ShapeDtypeStruct((n, d), F32),
        inputs=(x2, mod, norm_g.reshape(1, d), w_up, conv_w,
                conv_b.reshape(1, 2 * hidden), w_down, final_g.reshape(1, d)),
        scratch_shapes=[
            pltpu.VMEM((tm, d), BF16),
            pltpu.VMEM((tm, hidden), BF16),
            pltpu.VMEM((CONV_WIDTH - 1, 2 * hidden), F32),
        ],
        compiler_params=pltpu.CompilerParams(
            dimension_semantics=("arbitrary", "arbitrary"),
            vmem_limit_bytes=VMEM_LIMIT_BYTES),
        name="conv_ffn",
    )


def kernel(x, c, gm_w_in, gm_ln_g, gm_ln_b, gm_w_s, gm_b_s, gm_w_out, hg_w_in, hg_lb, hg_gn_g, hg_w_out, ffn_w_up, ffn_conv_w, ffn_conv_b, ffn_w_down, norm_g, ada_w, ada_b, final_g):
    batch, seq, d = x.shape
    depth = ada_w.shape[0]
    assert seq % TOKEN_TILE == 0 and TOKEN_TILE % GM_BLOCK == 0

    def mixer_weights(i):
        if i % 2 == 0:
            return ((gm_w_in, i // 2), (gm_w_out, i // 2))
        return ((hg_w_in, i // 2), (hg_w_out, i // 2))

    c_pad = jnp.pad(c, ((0, 8 - batch), (0, 0)))
    mod_all, w_a, w_b = _ada_modulation(c_pad, ada_w, ada_b, mixer_weights(0))
    mod_all = mod_all[:, :batch].reshape(depth, batch, 6, d)

    head_dim = gm_w_out.shape[1] // GM_HEADS
    x2 = x.reshape(batch * seq, d)
    for i in range(depth):
        j = i // 2
        ffn_weights = ((ffn_w_up, i), (ffn_w_down, i))
        if i % 2 == 0:
            b_s_tile = jnp.repeat(gm_b_s[j].T, head_dim, axis=1)
            x2, w_a, w_b = _gmlp_layer(
                x2, mod_all[i], norm_g[i, 0], w_a, gm_ln_g[j], gm_ln_b[j], gm_w_s,
                j, b_s_tile, w_b, seq, ffn_weights)
        else:
            x2, w_a, w_b = _hgrn_layer(
                x2, mod_all[i], norm_g[i, 0], w_a, hg_lb, j, hg_gn_g[j], w_b,
                batch, seq, ffn_weights)
        last = i == depth - 1
        x2, *next_mixer = _ffn_layer(
            x2, mod_all[i], norm_g[i, 1], w_a, ffn_conv_w, i, ffn_conv_b[i], w_b,
            final_g, batch, seq, final_norm=last,
            next_weights=() if last else mixer_weights(i + 1))
        if not last:
            w_a, w_b = next_mixer
    return x2.reshape(batch, seq, d)
```

```python
import functools

import jax
import jax.numpy as jnp
from jax import lax
from jax.experimental import pallas as pl
from jax.experimental.pallas import tpu as pltpu

F32 = jnp.float32
BF16 = jnp.bfloat16

EPS = 1e-6
CHUNK = 64
GM_BLOCK = 128
GM_HEADS = 8
HG_HEADS = 8
HG_DIM = 128
SCAN_CHUNK = 64
SCAN_HALVES = (32, 16, 8, 4, 2, 1)
CHUNKS_PER_GROUP = 1
SCAN_SCHEDULE = ("F0a S1a F1a S2a F2a S3a F3a M0a M1a B0a U0a M2a B1a U1a M3a B2a "
                 "U2a U3a B3a N0")
VREG_ROWS = 8
BF16_TILE_ROWS = 16
LOG2_E = 1.4426950408889634
CONV_WIDTH = 3
FFN_BLOCK = 256
SQRT_HALF = 0.7071067811865476

TOKEN_TILE = 512
VMEM_LIMIT_BYTES = 56 * 1024 * 1024


def _gelu(x):
    return 0.5 * x * (1.0 + lax.erf(x * SQRT_HALF))


def _silu(x):
    return x * (0.5 * jnp.tanh(0.5 * x) + 0.5)


def _modulated_rms_norm(x, gain, scale, shift):
    ms = jnp.mean(x * x, axis=-1, keepdims=True)
    return (x * lax.rsqrt(ms + EPS)) * (gain * (1.0 + scale)) + shift


def _resident(shape):
    zeros = (0,) * len(shape)
    return pl.BlockSpec(shape, lambda *_: zeros, pipeline_mode=pl.Buffered(1))


def _resident_layer(stacked_shape, layer):
    index = (layer,) + (0,) * (len(stacked_shape) - 1)
    return pl.BlockSpec((None,) + tuple(stacked_shape[1:]), lambda *_: index,
                        pipeline_mode=pl.Buffered(1))


def _cast_specs(stacked, layer, grid_steps, step_of):
    _, rows, cols = stacked.shape
    block = next(r for r in range(BF16_TILE_ROWS, rows + 1, BF16_TILE_ROWS)
                 if rows % r == 0 and r * grid_steps >= rows)
    last = rows // block - 1

    def block_index(*grid_indices):
        return jnp.minimum(step_of(*grid_indices), last)

    return (pl.BlockSpec((None, block, cols),
                         lambda *g: (layer, block_index(*g), 0)),
            pl.BlockSpec((block, cols), lambda *g: (block_index(*g), 0)),
            jax.ShapeDtypeStruct((rows, cols), BF16))


def _with_casts(body, n_inputs, n_casts):
    def kernel(*refs):
        out_at = n_inputs + n_casts
        for src, dst in zip(refs[n_inputs:out_at],
                            refs[out_at + 1:out_at + 1 + n_casts]):
            dst[...] = src[...].astype(BF16)
        body(*refs[:n_inputs], refs[out_at], *refs[out_at + 1 + n_casts:])
    return kernel


def _call_with_casts(body, n_inputs, next_weights, grid, step_of, in_specs,
                     out_spec, out_shape, inputs, **kwargs):
    steps = 1
    for extent in grid:
        steps *= extent
    casts = [_cast_specs(w, layer, steps, step_of) for w, layer in next_weights]
    return pl.pallas_call(
        _with_casts(body, n_inputs, len(casts)),
        grid=grid,
        in_specs=list(in_specs) + [c[0] for c in casts],
        out_specs=[out_spec] + [c[1] for c in casts],
        out_shape=[out_shape] + [c[2] for c in casts],
        **kwargs,
    )(*inputs, *[w for w, _ in next_weights])


def _ada_kernel(c_ref, w_ref, b_ref, o_ref):
    cond = _silu(c_ref[...])
    o_ref[0] = jnp.dot(cond, w_ref[0], preferred_element_type=F32) + b_ref[0]


def _ada_modulation(c_pad, ada_w, ada_b, next_weights):
    depth, d, n = ada_w.shape
    rows = c_pad.shape[0]
    tn = 1536
    return _call_with_casts(
        _ada_kernel, 3, next_weights,
        grid=(depth, n // tn),
        step_of=lambda l, j: l * (n // tn) + j,
        in_specs=[
            pl.BlockSpec((rows, d), lambda l, j: (0, 0)),
            pl.BlockSpec((1, d, tn), lambda l, j: (l, 0, j)),
            pl.BlockSpec((1, 1, tn), lambda l, j: (l, 0, j)),
        ],
        out_spec=pl.BlockSpec((1, rows, tn), lambda l, j: (l, 0, j)),
        out_shape=jax.ShapeDtypeStruct((depth, rows, n), F32),
        inputs=(c_pad, ada_w, ada_b.reshape(depth, 1, n)),
        compiler_params=pltpu.CompilerParams(
            dimension_semantics=("arbitrary", "arbitrary"),
            vmem_limit_bytes=VMEM_LIMIT_BYTES),
        name="ada_modulation",
    )


def _gmlp_kernel(x_ref, mod_ref, ng_ref, win_ref, lng_ref, lnb_ref, ws_ref,
                 bs_ref, wout_ref, o_ref, h_scr, u_scr, v_scr, vn_scr, gt_scr):
    tm, d = x_ref.shape
    width = u_scr.shape[1]
    head_dim = width // GM_HEADS
    x = x_ref[...]
    mod = mod_ref[0]
    h = _modulated_rms_norm(x, ng_ref[...], mod[1:2], mod[0:1])
    h_scr[...] = h.astype(BF16)

    cn = 512

    def store_gelu(dst_ref, j, z):
        dst_ref[:, j * cn:(j + 1) * cn] = _gelu(z)

    def layer_norm_v():
        v = v_scr[...]
        mu = jnp.mean(v, axis=-1, keepdims=True)
        dv = v - mu
        var = jnp.mean(dv * dv, axis=-1, keepdims=True)
        vn = dv * lax.rsqrt(var + EPS) * lng_ref[...] + lnb_ref[...]
        vn_scr[...] = vn.astype(BF16)

    row = lax.broadcasted_iota(jnp.int32, (GM_BLOCK, GM_BLOCK), 0)
    col = lax.broadcasted_iota(jnp.int32, (GM_BLOCK, GM_BLOCK), 1)
    allowed = (row >= CHUNK) | (col < CHUNK)

    def gate(hd):
        w = jnp.where(allowed, ws_ref[hd], 0.0).astype(BF16)
        cols = slice(hd * head_dim, (hd + 1) * head_dim)
        for r in range(tm // GM_BLOCK):
            rows = slice(r * GM_BLOCK, (r + 1) * GM_BLOCK)
            s = jnp.dot(w, vn_scr[rows, cols], preferred_element_type=F32)
            s = s + bs_ref[:, cols]
            gt_scr[rows, cols] = (u_scr[rows, cols] * s).astype(BF16)

    blocks = width // cn
    heads_per_block = cn // head_dim
    elementwise = []
    for dst_ref, first_col in ((v_scr, width), (u_scr, 0)):
        for j in range(blocks):
            cols = slice(first_col + j * cn, first_col + (j + 1) * cn)
            z = jnp.dot(h_scr[...], win_ref[:, cols], preferred_element_type=F32)
            for work in elementwise:
                work()
            elementwise = [functools.partial(store_gelu, dst_ref, j, z)]
            if dst_ref is v_scr and j == blocks - 1:
                elementwise.append(layer_norm_v)
            if dst_ref is u_scr:
                elementwise += [functools.partial(gate, hd) for hd in range(
                    j * heads_per_block, (j + 1) * heads_per_block)]
    for work in elementwise:
        work()
    y = jnp.dot(gt_scr[...], wout_ref[...], preferred_element_type=F32)
    o_ref[...] = x + mod[2:3] * y


def _gmlp_layer(x2, mod, norm_g, w_in, ln_g, ln_b, w_s, layer, b_s_tile, w_out,
                seq, next_weights):
    n, d = x2.shape
    tm = TOKEN_TILE
    width = w_out.shape[0]
    tiles_per_seq = seq // tm
    return _call_with_casts(
        _gmlp_kernel, 9, next_weights,
        grid=(n // tm,),
        step_of=lambda t: t,
        in_specs=[
            pl.BlockSpec((tm, d), lambda t: (t, 0)),
            pl.BlockSpec((1, 6, d), lambda t: (t // tiles_per_seq, 0, 0)),
            _resident((1, d)),
            _resident(w_in.shape),
            _resident((1, width)),
            _resident((1, width)),
            _resident_layer(w_s.shape, layer),
            _resident(b_s_tile.shape),
            _resident(w_out.shape),
        ],
        out_spec=pl.BlockSpec((tm, d), lambda t: (t, 0)),
        out_shape=jax.ShapeDtypeStruct((n, d), F32),
        inputs=(x2, mod, norm_g.reshape(1, d), w_in, ln_g.reshape(1, width),
                ln_b.reshape(1, width), w_s, b_s_tile, w_out),
        scratch_shapes=[
            pltpu.VMEM((tm, d), BF16),
            pltpu.VMEM((tm, width), F32),
            pltpu.VMEM((tm, width), F32),
            pltpu.VMEM((tm, width), BF16),
            pltpu.VMEM((tm, width), BF16),
        ],
        compiler_params=pltpu.CompilerParams(
            dimension_semantics=("arbitrary",),
            vmem_limit_bytes=VMEM_LIMIT_BYTES),
        name="gmlp_mixer",
    )


def _chunk_cumsum(g):
    groups = SCAN_CHUNK // VREG_ROWS
    g3 = g.reshape(groups, VREG_ROWS, g.shape[1])
    sub = lax.broadcasted_iota(jnp.int32, g3.shape, 1)
    step = 1
    while step < VREG_ROWS:
        g3 = g3 + jnp.where(sub >= step, pltpu.roll(g3, step, 1), 0.0)
        step *= 2
    pieces = [g3[0]]
    for r in range(1, groups):
        pieces.append(g3[r] + pieces[-1][VREG_ROWS - 1:VREG_ROWS, :])
    return jnp.concatenate(pieces, axis=0)


def _split_operands(q, k, cum, g, half):
    lanes = q.shape[1]
    if half >= VREG_ROWS:
        zeros = jnp.zeros((half, lanes), F32)
        q_parts, k_parts = [], []
        for start in range(0, SCAN_CHUNK, 2 * half):
            mid = start + half
            bnd = cum[mid - 1:mid, :]
            k_parts += [k[start:mid] * jnp.exp2(bnd - cum[start:mid]), zeros]
            q_parts += [zeros,
                        q[mid:mid + half] * jnp.exp2(cum[mid:mid + half] - bnd)]
        return jnp.concatenate(q_parts, axis=0), jnp.concatenate(k_parts, axis=0)
    if half == 1:
        return q * jnp.exp2(g), k
    groups = SCAN_CHUNK // VREG_ROWS
    cum3 = cum.reshape(groups, VREG_ROWS, lanes)
    sub = lax.broadcasted_iota(jnp.int32, cum3.shape, 1)
    bnd = None
    for start in range(0, VREG_ROWS, 2 * half):
        piece = jnp.broadcast_to(
            cum3[:, start + half - 1:start + half, :], cum3.shape)
        bnd = piece if bnd is None else jnp.where(sub >= start, piece, bnd)
    e = jnp.exp2(-jnp.abs(cum3 - bnd)).reshape(SCAN_CHUNK, lanes)
    return q * e, k * e


def _stage_head(rows, lanes, q_ref, k_ref, v_ref, g_ref, chunk, stage):
    ql_ref, ku_ref, qin_ref, kout_ref, v16_ref, direct_ref, decay_ref = stage
    q, k, v, g = (ref[rows, lanes] for ref in (q_ref, k_ref, v_ref, g_ref))
    cum = _chunk_cumsum(g)
    last = cum[SCAN_CHUNK - 1:SCAN_CHUNK, :]
    qin_ref[rows, lanes] = (q * jnp.exp2(cum)).astype(BF16)
    kout_ref[rows, lanes] = (k * jnp.exp2(last - cum)).astype(BF16)
    v16_ref[rows, lanes] = v.astype(BF16)
    decay_ref[chunk, :, lanes] = jnp.exp2(last)
    direct_ref[rows, lanes] = jnp.sum(q * k, axis=-1, keepdims=True) * v
    for idx, half in enumerate(SCAN_HALVES):
        q_l, k_u = _split_operands(q, k, cum, g, half)
        ql_ref[idx, rows, lanes] = q_l.astype(BF16)
        ku_ref[idx, rows, lanes] = k_u.astype(BF16)


def _block_diag(a, b):
    zeros = jnp.zeros_like(a)
    return jnp.concatenate([jnp.concatenate([a, zeros], axis=1),
                            jnp.concatenate([zeros, b], axis=1)], axis=0)


def _by_key_position(a, b):
    half = SCAN_CHUNK // 2
    return jnp.concatenate([_block_diag(a[:half], b[:half]),
                            _block_diag(a[half:], b[half:])], axis=0)


def _combine_head_pair(pair, rows, chunk, level_of_pair, stage, state_ref):
    ql_ref, ku_ref, qin_ref, kout_ref, v16_ref, direct_ref, decay_ref = stage
    nt = (((1,), (1,)), ((), ()))
    first, second = 2 * pair, 2 * pair + 1
    lanes_a = slice(first * HG_DIM, second * HG_DIM)
    lanes_b = slice(second * HG_DIM, (second + 1) * HG_DIM)
    lanes = slice(first * HG_DIM, (second + 1) * HG_DIM)

    def level_scores(idx):
        keys = _by_key_position(ku_ref[idx, rows, lanes_a],
                                ku_ref[idx, rows, lanes_b])
        if idx == 0:
            keys = keys[:SCAN_CHUNK]
        s = lax.dot_general(ql_ref[idx, rows, lanes], keys, nt,
                            preferred_element_type=F32)
        return jnp.concatenate([s, jnp.zeros_like(s)], axis=1) if idx == 0 else s

    level_scores = [level_scores(idx) for idx in range(len(SCAN_HALVES))]
    state_a, state_b = state_ref[first], state_ref[second]
    out = direct_ref[rows, lanes] + lax.dot_general(
        qin_ref[rows, lanes],
        _block_diag(state_a.astype(BF16), state_b.astype(BF16)), nt,
        preferred_element_type=F32)
    v_bd = _by_key_position(v16_ref[rows, lanes_a], v16_ref[rows, lanes_b])

    def advance_state():
        update = lax.dot_general(
            v_bd, _by_key_position(kout_ref[rows, lanes_a], kout_ref[rows, lanes_b]),
            (((0,), (0,)), ((), ())), preferred_element_type=F32)
        state_ref[first] = (state_a * decay_ref[chunk, :, lanes_a]
                            + update[:HG_DIM, :HG_DIM])
        state_ref[second] = (state_b * decay_ref[chunk, :, lanes_b]
                             + update[HG_DIM:, HG_DIM:])

    def finish():
        scores = level_scores[0]
        for half, s in zip(SCAN_HALVES[1:], level_scores[1:]):
            scores = jnp.where(level_of_pair[half], s, scores)
        weighted = jnp.dot(scores.astype(BF16), v_bd, preferred_element_type=F32)
        return lambda: out + weighted

    return finish, advance_state


def _hgrn_lower_bound(lb_raw, layer):
    e = jnp.exp(lb_raw - jnp.max(lb_raw, axis=0, keepdims=True))
    total = jnp.sum(e, axis=0, keepdims=True)
    below = jnp.zeros_like(total)
    for m in range(1, layer + 1):
        below = below + e[m:m + 1]
    return below / total


def _hgrn_kernel(x_ref, mod_ref, ng_ref, win_ref, lb_ref, gn_ref, wout_ref,
                 o_ref, h_scr, q_scr, k_scr, v_scr, lf_scr, sg_scr, o_scr,
                 state_scr, ql_scr, ku_scr, qin_scr, kout_scr, v16_scr,
                 decay_scr, *, layer):
    tm, d = x_ref.shape
    stage = (ql_scr, ku_scr, qin_scr, kout_scr, v16_scr, o_scr, decay_scr)

    @pl.when(pl.program_id(1) == 0)
    def _():
        state_scr[...] = jnp.zeros_like(state_scr)

    x = x_ref[...]
    mod = mod_ref[0]
    h = _modulated_rms_norm(x, ng_ref[...], mod[1:2], mod[0:1])
    h_scr[...] = h.astype(BF16)

    def proj(i):
        return jnp.dot(h_scr[...], win_ref[:, i * d:(i + 1) * d],
                       preferred_element_type=F32)

    lb = _hgrn_lower_bound(lb_ref[...], layer)
    q_lin = proj(0)
    f_lin = proj(1)
    q_scr[...] = _silu(q_lin)
    v_scr[...] = proj(2)
    f = lb + (1.0 - lb) * jax.nn.sigmoid(f_lin)
    lf_scr[...] = jnp.log(f) * LOG2_E
    k_scr[...] = 1.0 - f
    sg_scr[...] = _silu(proj(3))

    head_lanes = [slice(hd * HG_DIM, (hd + 1) * HG_DIM)
                  for hd in range(HG_HEADS)]

    def chunk_rows(c):
        start = c * SCAN_CHUNK
        if not isinstance(c, int):
            start = pl.multiple_of(start, SCAN_CHUNK)
        return pl.ds(start, SCAN_CHUNK)

    def normalize_pair(rows, pair, out):
        for hd in (2 * pair, 2 * pair + 1):
            o = out[:, (hd % 2) * HG_DIM:(hd % 2 + 1) * HG_DIM]
            ms = jnp.mean(o * o, axis=-1, keepdims=True)
            o_scr[rows, head_lanes[hd]] = o * lax.rsqrt(ms + EPS)

    def stage_pair(c, pair):
        for hd in (2 * pair, 2 * pair + 1):
            _stage_head(chunk_rows(c), head_lanes[hd], q_scr, k_scr, v_scr,
                        lf_scr, c, stage)

    def chunk_group_body(group, carry, schedule):
        first_chunk = group * CHUNKS_PER_GROUP
        half_chunk = SCAN_CHUNK // 2
        row_s = lax.broadcasted_iota(jnp.int32, (SCAN_CHUNK, 2 * SCAN_CHUNK), 0)
        lane_s = lax.broadcasted_iota(jnp.int32, (SCAN_CHUNK, 2 * SCAN_CHUNK), 1)
        col_s = ((lane_s >> (SCAN_CHUNK.bit_length() - 1)) * half_chunk
                 + (lane_s & (half_chunk - 1)))
        level_of_pair = {
            half: (((row_s ^ col_s) >> (half.bit_length() - 1)) == 1)
            & (row_s > col_s) for half in SCAN_HALVES[1:]}
        finish, advance_state, emit = {}, {}, {}
        for step in schedule.split():
            kind, pair = step[0], int(step[1])
            if kind == "N":
                stage_pair(first_chunk + CHUNKS_PER_GROUP, pair)
                continue
            c = first_chunk + "ab".index(step[2])
            item = step[1:]
            if kind == "S":
                stage_pair(c, pair)
            elif kind == "F":
                finish[item], advance_state[item] = _combine_head_pair(
                    pair, chunk_rows(c), c, level_of_pair, stage, state_scr)
            elif kind == "M":
                emit[item] = finish[item]()
            elif kind == "B":
                normalize_pair(chunk_rows(c), pair, emit[item]())
            else:
                advance_state[item]()
        return carry

    groups = tm // (SCAN_CHUNK * CHUNKS_PER_GROUP)
    stage_pair(0, 0)
    lax.fori_loop(0, groups - 1,
                  functools.partial(chunk_group_body, schedule=SCAN_SCHEDULE), 0)
    chunk_group_body(groups - 1, 0, SCAN_SCHEDULE.replace(" N0", ""))

    gated = (o_scr[...] * gn_ref[...]) * sg_scr[...]
    y = jnp.dot(gated.astype(BF16), wout_ref[...], preferred_element_type=F32)
    o_ref[...] = x + mod[2:3] * y


def _hgrn_layer(x2, mod, norm_g, w_in, lb_raw, layer, gn_g, w_out, batch, seq,
                next_weights):
    n, d = x2.shape
    tm = TOKEN_TILE
    tiles_per_seq = seq // tm
    act = pltpu.VMEM((tm, d), F32)
    return _call_with_casts(
        functools.partial(_hgrn_kernel, layer=layer), 7, next_weights,
        grid=(batch, tiles_per_seq),
        step_of=lambda b, t: b * tiles_per_seq + t,
        in_specs=[
            pl.BlockSpec((tm, d), lambda b, t: (b * tiles_per_seq + t, 0)),
            pl.BlockSpec((1, 6, d), lambda b, t: (b, 0, 0)),
            _resident((1, d)),
            _resident(w_in.shape),
            _resident(lb_raw.shape),
            _resident((1, d)),
            _resident(w_out.shape),
        ],
        out_spec=pl.BlockSpec((tm, d), lambda b, t: (b * tiles_per_seq + t, 0)),
        out_shape=jax.ShapeDtypeStruct((n, d), F32),
        inputs=(x2, mod, norm_g.reshape(1, d), w_in, lb_raw.astype(F32),
                gn_g.reshape(1, d), w_out),
        scratch_shapes=[
            pltpu.VMEM((tm, d), BF16),
            act, act, act, act, act, act,
            pltpu.VMEM((HG_HEADS, HG_DIM, HG_DIM), F32),
            pltpu.VMEM((len(SCAN_HALVES), tm, d), BF16),
            pltpu.VMEM((len(SCAN_HALVES), tm, d), BF16),
            pltpu.VMEM((tm, d), BF16),
            pltpu.VMEM((tm, d), BF16),
            pltpu.VMEM((tm, d), BF16),
            pltpu.VMEM((tm // SCAN_CHUNK, 1, d), F32),
        ],
        compiler_params=pltpu.CompilerParams(
            dimension_semantics=("arbitrary", "arbitrary"),
            vmem_limit_bytes=VMEM_LIMIT_BYTES),
        name="hgrn_mixer",
    )


def _ffn_kernel(x_ref, mod_ref, ng_ref, wup_ref, cw_ref, cb_ref, wdown_ref,
                fg_ref, o_ref, h_scr, p_scr, halo_scr, *, final_norm):
    tm, d = x_ref.shape
    hidden = wdown_ref.shape[0]

    @pl.when(pl.program_id(1) == 0)
    def _():
        halo_scr[...] = jnp.zeros_like(halo_scr)

    x = x_ref[...]
    mod = mod_ref[0]
    h = _modulated_rms_norm(x, ng_ref[...], mod[4:5], mod[3:4])
    h_scr[...] = h.astype(BF16)

    cn = FFN_BLOCK
    row = lax.broadcasted_iota(jnp.int32, (tm, cn), 0)

    def conv(cols):
        a = jnp.dot(h_scr[...], wup_ref[:, cols], preferred_element_type=F32)
        prev1 = halo_scr[1:2, cols]
        prev2 = halo_scr[0:1, cols]
        halo_scr[:, cols] = a[tm - 2:tm, :]
        a1 = jnp.where(row == 0, prev1, pltpu.roll(a, 1, 0))
        a2 = jnp.where(row == 0, prev2,
                       jnp.where(row == 1, prev1, pltpu.roll(a, 2, 0)))
        return (SQRT_HALF * cb_ref[:, cols]
                + (SQRT_HALF * cw_ref[0:1, cols]) * a2
                + (SQRT_HALF * cw_ref[1:2, cols]) * a1
                + (SQRT_HALF * cw_ref[2:3, cols]) * a)

    for j in range(hidden // cn):
        gate = conv(slice(j * cn, (j + 1) * cn))
        val = conv(slice(hidden + j * cn, hidden + (j + 1) * cn))
        p_scr[:, j * cn:(j + 1) * cn] = (
            gate * (1.0 + lax.erf(gate)) * val).astype(BF16)

    y = jnp.dot(p_scr[...], wdown_ref[...], preferred_element_type=F32)
    out = x + mod[5:6] * y
    if final_norm:
        ms = jnp.mean(out * out, axis=-1, keepdims=True)
        out = (out * lax.rsqrt(ms + EPS)) * fg_ref[...]
    o_ref[...] = out


def _ffn_layer(x2, mod, norm_g, w_up, conv_w, layer, conv_b, w_down, final_g,
               batch, seq, final_norm, next_weights):
    n, d = x2.shape
    tm = TOKEN_TILE
    tiles_per_seq = seq // tm
    hidden = w_down.shape[0]
    return _call_with_casts(
        functools.partial(_ffn_kernel, final_norm=final_norm), 8, next_weights,
        grid=(batch, tiles_per_seq),
        step_of=lambda b, t: b * tiles_per_seq + t,
        in_specs=[
            pl.BlockSpec((tm, d), lambda b, t: (b * tiles_per_seq + t, 0)),
            pl.BlockSpec((1, 6, d), lambda b, t: (b, 0, 0)),
            _resident((1, d)),
            _resident(w_up.shape),
            _resident_layer(conv_w.shape, layer),
            _resident((1, 2 * hidden)),
            _resident(w_down.shape),
            _resident((1, d)),
        ],
        out_spec=pl.BlockSpec((tm, d), lambda b, t: (b * tiles_per_seq + t, 0)),
        out_shape=jax.ShapeDtypeStruct((n, d), F32),
        inputs=(x2, mod, norm_g.reshape(1, d), w_up, conv_w,
                conv_b.reshape(1, 2 * hidden), w_down, final_g.reshape(1, d)),
        scratch_shapes=[
            pltpu.VMEM((tm, d), BF16),
            pltpu.VMEM((tm, hidden), BF16),
            pltpu.VMEM((CONV_WIDTH - 1, 2 * hidden), F32),
        ],
        compiler_params=pltpu.CompilerParams(
            dimension_semantics=("arbitrary", "arbitrary"),
            vmem_limit_bytes=VMEM_LIMIT_BYTES),
        name="conv_ffn",
    )


def kernel(x, c, gm_w_in, gm_ln_g, gm_ln_b, gm_w_s, gm_b_s, gm_w_out, hg_w_in, hg_lb, hg_gn_g, hg_w_out, ffn_w_up, ffn_conv_w, ffn_conv_b, ffn_w_down, norm_g, ada_w, ada_b, final_g):
    batch, seq, d = x.shape
    depth = ada_w.shape[0]
    assert seq % TOKEN_TILE == 0 and TOKEN_TILE % GM_BLOCK == 0

    def mixer_weights(i):
        if i % 2 == 0:
            return ((gm_w_in, i // 2), (gm_w_out, i // 2))
        return ((hg_w_in, i // 2), (hg_w_out, i // 2))

    c_pad = jnp.pad(c, ((0, 8 - batch), (0, 0)))
    mod_all, w_a, w_b = _ada_modulation(c_pad, ada_w, ada_b, mixer_weights(0))
    mod_all = mod_all[:, :batch].reshape(depth, batch, 6, d)

    head_dim = gm_w_out.shape[1] // GM_HEADS
    x2 = x.reshape(batch * seq, d)
    for i in range(depth):
        j = i // 2
        ffn_weights = ((ffn_w_up, i), (ffn_w_down, i))
        if i % 2 == 0:
            b_s_tile = jnp.repeat(gm_b_s[j].T, head_dim, axis=1)
            x2, w_a, w_b = _gmlp_layer(
                x2, mod_all[i], norm_g[i, 0], w_a, gm_ln_g[j], gm_ln_b[j], gm_w_s,
                j, b_s_tile, w_b, seq, ffn_weights)
        else:
            x2, w_a, w_b = _hgrn_layer(
                x2, mod_all[i], norm_g[i, 0], w_a, hg_lb, j, hg_gn_g[j], w_b,
                batch, seq, ffn_weights)
        last = i == depth - 1
        x2, *next_mixer = _ffn_layer(
            x2, mod_all[i], norm_g[i, 1], w_a, ffn_conv_w, i, ffn_conv_b[i], w_b,
            final_g, batch, seq, final_norm=last,
            next_weights=() if last else mixer_weights(i + 1))
        if not last:
            w_a, w_b = next_mixer
    return x2.reshape(batch, seq, d)
```

```python
import functools

import jax
import jax.numpy as jnp
from jax import lax
from jax.experimental import pallas as pl
from jax.experimental.pallas import tpu as pltpu

F32 = jnp.float32
BF16 = jnp.bfloat16

EPS = 1e-6
CHUNK = 64
GM_BLOCK = 128
GM_HEADS = 8
HG_HEADS = 8
HG_DIM = 128
SCAN_CHUNK = 64
SCAN_HALVES = (32, 16, 8, 4, 2)
CHUNKS_PER_GROUP = 1
SCAN_SCHEDULE = ("F0a S1a F1a S2a F2a S3a F3a M0a M1a B0a U0a M2a B1a U1a M3a B2a "
                 "U2a U3a B3a N0")
VREG_ROWS = 8
BF16_TILE_ROWS = 16
LOG2_E = 1.4426950408889634
CONV_WIDTH = 3
FFN_BLOCK = 256
SQRT_HALF = 0.7071067811865476

TOKEN_TILE = 512
VMEM_LIMIT_BYTES = 56 * 1024 * 1024


def _gelu(x):
    return 0.5 * x * (1.0 + lax.erf(x * SQRT_HALF))


def _silu(x):
    return x * (0.5 * jnp.tanh(0.5 * x) + 0.5)


def _modulated_rms_norm(x, gain, scale, shift):
    ms = jnp.mean(x * x, axis=-1, keepdims=True)
    return (x * lax.rsqrt(ms + EPS)) * (gain * (1.0 + scale)) + shift


def _resident(shape):
    zeros = (0,) * len(shape)
    return pl.BlockSpec(shape, lambda *_: zeros, pipeline_mode=pl.Buffered(1))


def _resident_layer(stacked_shape, layer):
    index = (layer,) + (0,) * (len(stacked_shape) - 1)
    return pl.BlockSpec((None,) + tuple(stacked_shape[1:]), lambda *_: index,
                        pipeline_mode=pl.Buffered(1))


def _cast_specs(stacked, layer, grid_steps, step_of):
    _, rows, cols = stacked.shape
    block = next(r for r in range(BF16_TILE_ROWS, rows + 1, BF16_TILE_ROWS)
                 if rows % r == 0 and r * grid_steps >= rows)
    last = rows // block - 1

    def block_index(*grid_indices):
        return jnp.minimum(step_of(*grid_indices), last)

    return (pl.BlockSpec((None, block, cols),
                         lambda *g: (layer, block_index(*g), 0)),
            pl.BlockSpec((block, cols), lambda *g: (block_index(*g), 0)),
            jax.ShapeDtypeStruct((rows, cols), BF16))


def _with_casts(body, n_inputs, n_casts):
    def kernel(*refs):
        out_at = n_inputs + n_casts
        for src, dst in zip(refs[n_inputs:out_at],
                            refs[out_at + 1:out_at + 1 + n_casts]):
            dst[...] = src[...].astype(BF16)
        body(*refs[:n_inputs], refs[out_at], *refs[out_at + 1 + n_casts:])
    return kernel


def _call_with_casts(body, n_inputs, next_weights, grid, step_of, in_specs,
                     out_spec, out_shape, inputs, **kwargs):
    steps = 1
    for extent in grid:
        steps *= extent
    casts = [_cast_specs(w, layer, steps, step_of) for w, layer in next_weights]
    return pl.pallas_call(
        _with_casts(body, n_inputs, len(casts)),
        grid=grid,
        in_specs=list(in_specs) + [c[0] for c in casts],
        out_specs=[out_spec] + [c[1] for c in casts],
        out_shape=[out_shape] + [c[2] for c in casts],
        **kwargs,
    )(*inputs, *[w for w, _ in next_weights])


def _ada_kernel(c_ref, w_ref, b_ref, o_ref):
    cond = _silu(c_ref[...])
    o_ref[0] = jnp.dot(cond, w_ref[0], preferred_element_type=F32) + b_ref[0]


def _ada_modulation(c_pad, ada_w, ada_b, next_weights):
    depth, d, n = ada_w.shape
    rows = c_pad.shape[0]
    tn = 1536
    return _call_with_casts(
        _ada_kernel, 3, next_weights,
        grid=(depth, n // tn),
        step_of=lambda l, j: l * (n // tn) + j,
        in_specs=[
            pl.BlockSpec((rows, d), lambda l, j: (0, 0)),
            pl.BlockSpec((1, d, tn), lambda l, j: (l, 0, j)),
            pl.BlockSpec((1, 1, tn), lambda l, j: (l, 0, j)),
        ],
        out_spec=pl.BlockSpec((1, rows, tn), lambda l, j: (l, 0, j)),
        out_shape=jax.ShapeDtypeStruct((depth, rows, n), F32),
        inputs=(c_pad, ada_w, ada_b.reshape(depth, 1, n)),
        compiler_params=pltpu.CompilerParams(
            dimension_semantics=("arbitrary", "arbitrary"),
            vmem_limit_bytes=VMEM_LIMIT_BYTES),
        name="ada_modulation",
    )


def _gmlp_kernel(x_ref, mod_ref, ng_ref, win_ref, lng_ref, lnb_ref, ws_ref,
                 bs_ref, wout_ref, o_ref, h_scr, u_scr, v_scr, vn_scr, gt_scr):
    tm, d = x_ref.shape
    width = u_scr.shape[1]
    head_dim = width // GM_HEADS
    x = x_ref[...]
    mod = mod_ref[0]
    h = _modulated_rms_norm(x, ng_ref[...], mod[1:2], mod[0:1])
    h_scr[...] = h.astype(BF16)

    cn = 512

    def store_gelu(dst_ref, j, z):
        dst_ref[:, j * cn:(j + 1) * cn] = _gelu(z)

    def layer_norm_v():
        v = v_scr[...]
        mu = jnp.mean(v, axis=-1, keepdims=True)
        dv = v - mu
        var = jnp.mean(dv * dv, axis=-1, keepdims=True)
        vn = dv * lax.rsqrt(var + EPS) * lng_ref[...] + lnb_ref[...]
        vn_scr[...] = vn.astype(BF16)

    row = lax.broadcasted_iota(jnp.int32, (GM_BLOCK, GM_BLOCK), 0)
    col = lax.broadcasted_iota(jnp.int32, (GM_BLOCK, GM_BLOCK), 1)
    allowed = (row >= CHUNK) | (col < CHUNK)

    def gate(hd):
        w = jnp.where(allowed, ws_ref[hd], 0.0).astype(BF16)
        cols = slice(hd * head_dim, (hd + 1) * head_dim)
        for r in range(tm // GM_BLOCK):
            rows = slice(r * GM_BLOCK, (r + 1) * GM_BLOCK)
            s = jnp.dot(w, vn_scr[rows, cols], preferred_element_type=F32)
            s = s + bs_ref[:, cols]
            gt_scr[rows, cols] = (u_scr[rows, cols] * s).astype(BF16)

    blocks = width // cn
    heads_per_block = cn // head_dim
    elementwise = []
    for dst_ref, first_col in ((v_scr, width), (u_scr, 0)):
        for j in range(blocks):
            cols = slice(first_col + j * cn, first_col + (j + 1) * cn)
            z = jnp.dot(h_scr[...], win_ref[:, cols], preferred_element_type=F32)
            for work in elementwise:
                work()
            elementwise = [functools.partial(store_gelu, dst_ref, j, z)]
            if dst_ref is v_scr and j == blocks - 1:
                elementwise.append(layer_norm_v)
            if dst_ref is u_scr:
                elementwise += [functools.partial(gate, hd) for hd in range(
                    j * heads_per_block, (j + 1) * heads_per_block)]
    for work in elementwise:
        work()
    y = jnp.dot(gt_scr[...], wout_ref[...], preferred_element_type=F32)
    o_ref[...] = x + mod[2:3] * y


def _gmlp_layer(x2, mod, norm_g, w_in, ln_g, ln_b, w_s, layer, b_s_tile, w_out,
                seq, next_weights):
    n, d = x2.shape
    tm = TOKEN_TILE
    width = w_out.shape[0]
    tiles_per_seq = seq // tm
    return _call_with_casts(
        _gmlp_kernel, 9, next_weights,
        grid=(n // tm,),
        step_of=lambda t: t,
        in_specs=[
            pl.BlockSpec((tm, d), lambda t: (t, 0)),
            pl.BlockSpec((1, 6, d), lambda t: (t // tiles_per_seq, 0, 0)),
            _resident((1, d)),
            _resident(w_in.shape),
            _resident((1, width)),
            _resident((1, width)),
            _resident_layer(w_s.shape, layer),
            _resident(b_s_tile.shape),
            _resident(w_out.shape),
        ],
        out_spec=pl.BlockSpec((tm, d), lambda t: (t, 0)),
        out_shape=jax.ShapeDtypeStruct((n, d), F32),
        inputs=(x2, mod, norm_g.reshape(1, d), w_in, ln_g.reshape(1, width),
                ln_b.reshape(1, width), w_s, b_s_tile, w_out),
        scratch_shapes=[
            pltpu.VMEM((tm, d), BF16),
            pltpu.VMEM((tm, width), F32),
            pltpu.VMEM((tm, width), F32),
            pltpu.VMEM((tm, width), BF16),
            pltpu.VMEM((tm, width), BF16),
        ],
        compiler_params=pltpu.CompilerParams(
            dimension_semantics=("arbitrary",),
            vmem_limit_bytes=VMEM_LIMIT_BYTES),
        name="gmlp_mixer",
    )


def _chunk_cumsum(g):
    groups = SCAN_CHUNK // VREG_ROWS
    g3 = g.reshape(groups, VREG_ROWS, g.shape[1])
    sub = lax.broadcasted_iota(jnp.int32, g3.shape, 1)
    step = 1
    while step < VREG_ROWS:
        g3 = g3 + jnp.where(sub >= step, pltpu.roll(g3, step, 1), 0.0)
        step *= 2
    pieces = [g3[0]]
    for r in range(1, groups):
        pieces.append(g3[r] + pieces[-1][VREG_ROWS - 1:VREG_ROWS, :])
    return jnp.concatenate(pieces, axis=0)


def _split_operands(q, k, cum, g, half):
    lanes = q.shape[1]
    if half >= VREG_ROWS:
        zeros = jnp.zeros((half, lanes), F32)
        q_parts, k_parts = [], []
        for start in range(0, SCAN_CHUNK, 2 * half):
            mid = start + half
            bnd = cum[mid - 1:mid, :]
            k_parts += [k[start:mid] * jnp.exp2(bnd - cum[start:mid]), zeros]
            q_parts += [zeros,
                        q[mid:mid + half] * jnp.exp2(cum[mid:mid + half] - bnd)]
        return jnp.concatenate(q_parts, axis=0), jnp.concatenate(k_parts, axis=0)
    groups = SCAN_CHUNK // VREG_ROWS
    cum3 = cum.reshape(groups, VREG_ROWS, lanes)
    sub = lax.broadcasted_iota(jnp.int32, cum3.shape, 1)
    bnd = None
    for start in range(0, VREG_ROWS, 2 * half):
        piece = jnp.broadcast_to(
            cum3[:, start + half - 1:start + half, :], cum3.shape)
        bnd = piece if bnd is None else jnp.where(sub >= start, piece, bnd)
    e = jnp.exp2(-jnp.abs(cum3 - bnd)).reshape(SCAN_CHUNK, lanes)
    return q * e, k * e


def _stage_head(rows, lanes, q_ref, k_ref, v_ref, g_ref, chunk, stage):
    ql_ref, ku_ref, qin_ref, kout_ref, v16_ref, direct_ref, decay_ref = stage
    q, k, v, g = (ref[rows, lanes] for ref in (q_ref, k_ref, v_ref, g_ref))
    cum = _chunk_cumsum(g)
    last = cum[SCAN_CHUNK - 1:SCAN_CHUNK, :]
    qin_ref[rows, lanes] = (q * jnp.exp2(cum)).astype(BF16)
    kout_ref[rows, lanes] = (k * jnp.exp2(last - cum)).astype(BF16)
    v16_ref[rows, lanes] = v.astype(BF16)
    decay_ref[chunk, :, lanes] = jnp.exp2(last)
    groups = SCAN_CHUNK // VREG_ROWS

    def previous_row(a):
        a3 = a.reshape(groups, VREG_ROWS, a.shape[1])
        return pltpu.roll(a3, 1, 1).reshape(a.shape)

    odd = (lax.broadcasted_iota(jnp.int32, q.shape, 0) & 1) == 1
    diagonal = jnp.sum(q * k, axis=-1, keepdims=True)
    adjacent = jnp.sum(jnp.where(odd, q * previous_row(k) * jnp.exp2(g), 0.0),
                       axis=-1, keepdims=True)
    direct_ref[rows, lanes] = diagonal * v + adjacent * previous_row(v)
    for idx, half in enumerate(SCAN_HALVES):
        q_l, k_u = _split_operands(q, k, cum, g, half)
        ql_ref[idx, rows, lanes] = q_l.astype(BF16)
        ku_ref[idx, rows, lanes] = k_u.astype(BF16)


def _block_diag(a, b):
    zeros = jnp.zeros_like(a)
    return jnp.concatenate([jnp.concatenate([a, zeros], axis=1),
                            jnp.concatenate([zeros, b], axis=1)], axis=0)


def _by_key_position(a, b):
    half = SCAN_CHUNK // 2
    return jnp.concatenate([_block_diag(a[:half], b[:half]),
                            _block_diag(a[half:], b[half:])], axis=0)


def _combine_head_pair(pair, rows, chunk, level_of_pair, stage, state_ref):
    ql_ref, ku_ref, qin_ref, kout_ref, v16_ref, direct_ref, decay_ref = stage
    nt = (((1,), (1,)), ((), ()))
    first, second = 2 * pair, 2 * pair + 1
    lanes_a = slice(first * HG_DIM, second * HG_DIM)
    lanes_b = slice(second * HG_DIM, (second + 1) * HG_DIM)
    lanes = slice(first * HG_DIM, (second + 1) * HG_DIM)

    def level_scores(idx):
        keys = _by_key_position(ku_ref[idx, rows, lanes_a],
                                ku_ref[idx, rows, lanes_b])
        if idx == 0:
            keys = keys[:SCAN_CHUNK]
        s = lax.dot_general(ql_ref[idx, rows, lanes], keys, nt,
                            preferred_element_type=F32)
        return jnp.concatenate([s, jnp.zeros_like(s)], axis=1) if idx == 0 else s

    level_scores = [level_scores(idx) for idx in range(len(SCAN_HALVES))]
    state_a, state_b = state_ref[first], state_ref[second]
    out = direct_ref[rows, lanes] + lax.dot_general(
        qin_ref[rows, lanes],
        _block_diag(state_a.astype(BF16), state_b.astype(BF16)), nt,
        preferred_element_type=F32)
    v_bd = _by_key_position(v16_ref[rows, lanes_a], v16_ref[rows, lanes_b])

    def advance_state():
        update = lax.dot_general(
            v_bd, _by_key_position(kout_ref[rows, lanes_a], kout_ref[rows, lanes_b]),
            (((0,), (0,)), ((), ())), preferred_element_type=F32)
        state_ref[first] = (state_a * decay_ref[chunk, :, lanes_a]
                            + update[:HG_DIM, :HG_DIM])
        state_ref[second] = (state_b * decay_ref[chunk, :, lanes_b]
                             + update[HG_DIM:, HG_DIM:])

    def finish():
        scores = level_scores[0]
        for half, s in zip(SCAN_HALVES[1:], level_scores[1:]):
            scores = jnp.where(level_of_pair[half], s, scores)
        weighted = jnp.dot(scores.astype(BF16), v_bd, preferred_element_type=F32)
        return lambda: out + weighted

    return finish, advance_state


def _hgrn_lower_bound(lb_raw, layer):
    e = jnp.exp(lb_raw - jnp.max(lb_raw, axis=0, keepdims=True))
    total = jnp.sum(e, axis=0, keepdims=True)
    below = jnp.zeros_like(total)
    for m in range(1, layer + 1):
        below = below + e[m:m + 1]
    return below / total


def _hgrn_kernel(x_ref, mod_ref, ng_ref, win_ref, lb_ref, gn_ref, wout_ref,
                 o_ref, h_scr, q_scr, k_scr, v_scr, lf_scr, sg_scr, o_scr,
                 state_scr, ql_scr, ku_scr, qin_scr, kout_scr, v16_scr,
                 decay_scr, *, layer):
    tm, d = x_ref.shape
    stage = (ql_scr, ku_scr, qin_scr, kout_scr, v16_scr, o_scr, decay_scr)

    @pl.when(pl.program_id(1) == 0)
    def _():
        state_scr[...] = jnp.zeros_like(state_scr)

    x = x_ref[...]
    mod = mod_ref[0]
    h = _modulated_rms_norm(x, ng_ref[...], mod[1:2], mod[0:1])
    h_scr[...] = h.astype(BF16)

    def proj(i):
        return jnp.dot(h_scr[...], win_ref[:, i * d:(i + 1) * d],
                       preferred_element_type=F32)

    lb = _hgrn_lower_bound(lb_ref[...], layer)
    q_lin = proj(0)
    f_lin = proj(1)
    q_scr[...] = _silu(q_lin)
    v_scr[...] = proj(2)
    f = lb + (1.0 - lb) * jax.nn.sigmoid(f_lin)
    lf_scr[...] = jnp.log(f) * LOG2_E
    k_scr[...] = 1.0 - f
    sg_scr[...] = _silu(proj(3))

    head_lanes = [slice(hd * HG_DIM, (hd + 1) * HG_DIM)
                  for hd in range(HG_HEADS)]

    def chunk_rows(c):
        start = c * SCAN_CHUNK
        if not isinstance(c, int):
            start = pl.multiple_of(start, SCAN_CHUNK)
        return pl.ds(start, SCAN_CHUNK)

    def normalize_pair(rows, pair, out):
        for hd in (2 * pair, 2 * pair + 1):
            o = out[:, (hd % 2) * HG_DIM:(hd % 2 + 1) * HG_DIM]
            ms = jnp.mean(o * o, axis=-1, keepdims=True)
            o_scr[rows, head_lanes[hd]] = o * lax.rsqrt(ms + EPS)

    def stage_pair(c, pair):
        for hd in (2 * pair, 2 * pair + 1):
            _stage_head(chunk_rows(c), head_lanes[hd], q_scr, k_scr, v_scr,
                        lf_scr, c, stage)

    def chunk_group_body(group, carry, schedule):
        first_chunk = group * CHUNKS_PER_GROUP
        half_chunk = SCAN_CHUNK // 2
        row_s = lax.broadcasted_iota(jnp.int32, (SCAN_CHUNK, 2 * SCAN_CHUNK), 0)
        lane_s = lax.broadcasted_iota(jnp.int32, (SCAN_CHUNK, 2 * SCAN_CHUNK), 1)
        col_s = ((lane_s >> (SCAN_CHUNK.bit_length() - 1)) * half_chunk
                 + (lane_s & (half_chunk - 1)))
        level_of_pair = {
            half: (((row_s ^ col_s) >> (half.bit_length() - 1)) == 1)
            & (row_s > col_s) for half in SCAN_HALVES[1:]}
        finish, advance_state, emit = {}, {}, {}
        for step in schedule.split():
            kind, pair = step[0], int(step[1])
            if kind == "N":
                stage_pair(first_chunk + CHUNKS_PER_GROUP, pair)
                continue
            c = first_chunk + "ab".index(step[2])
            item = step[1:]
            if kind == "S":
                stage_pair(c, pair)
            elif kind == "F":
                finish[item], advance_state[item] = _combine_head_pair(
                    pair, chunk_rows(c), c, level_of_pair, stage, state_scr)
            elif kind == "M":
                emit[item] = finish[item]()
            elif kind == "B":
                normalize_pair(chunk_rows(c), pair, emit[item]())
            else:
                advance_state[item]()
        return carry

    groups = tm // (SCAN_CHUNK * CHUNKS_PER_GROUP)
    stage_pair(0, 0)
    lax.fori_loop(0, groups - 1,
                  functools.partial(chunk_group_body, schedule=SCAN_SCHEDULE), 0)
    chunk_group_body(groups - 1, 0, SCAN_SCHEDULE.replace(" N0", ""))

    gated = (o_scr[...] * gn_ref[...]) * sg_scr[...]
    y = jnp.dot(gated.astype(BF16), wout_ref[...], preferred_element_type=F32)
    o_ref[...] = x + mod[2:3] * y


def _hgrn_layer(x2, mod, norm_g, w_in, lb_raw, layer, gn_g, w_out, batch, seq,
                next_weights):
    n, d = x2.shape
    tm = TOKEN_TILE
    tiles_per_seq = seq // tm
    act = pltpu.VMEM((tm, d), F32)
    return _call_with_casts(
        functools.partial(_hgrn_kernel, layer=layer), 7, next_weights,
        grid=(batch, tiles_per_seq),
        step_of=lambda b, t: b * tiles_per_seq + t,
        in_specs=[
            pl.BlockSpec((tm, d), lambda b, t: (b * tiles_per_seq + t, 0)),
            pl.BlockSpec((1, 6, d), lambda b, t: (b, 0, 0)),
            _resident((1, d)),
            _resident(w_in.shape),
            _resident(lb_raw.shape),
            _resident((1, d)),
            _resident(w_out.shape),
        ],
        out_spec=pl.BlockSpec((tm, d), lambda b, t: (b * tiles_per_seq + t, 0)),
        out_shape=jax.ShapeDtypeStruct((n, d), F32),
        inputs=(x2, mod, norm_g.reshape(1, d), w_in, lb_raw.astype(F32),
                gn_g.reshape(1, d), w_out),
        scratch_shapes=[
            pltpu.VMEM((tm, d), BF16),
            act, act, act, act, act, act,
            pltpu.VMEM((HG_HEADS, HG_DIM, HG_DIM), F32),
            pltpu.VMEM((len(SCAN_HALVES), tm, d), BF16),
            pltpu.VMEM((len(SCAN_HALVES), tm, d), BF16),
            pltpu.VMEM((tm, d), BF16),
            pltpu.VMEM((tm, d), BF16),
            pltpu.VMEM((tm, d), BF16),
            pltpu.VMEM((tm // SCAN_CHUNK, 1, d), F32),
        ],
        compiler_params=pltpu.CompilerParams(
            dimension_semantics=("arbitrary", "arbitrary"),
            vmem_limit_bytes=VMEM_LIMIT_BYTES),
        name="hgrn_mixer",
    )


def _ffn_kernel(x_ref, mod_ref, ng_ref, wup_ref, cw_ref, cb_ref, wdown_ref,
                fg_ref, o_ref, h_scr, p_scr, halo_scr, *, final_norm):
    tm, d = x_ref.shape
    hidden = wdown_ref.shape[0]

    @pl.when(pl.program_id(1) == 0)
    def _():
        halo_scr[...] = jnp.zeros_like(halo_scr)

    x = x_ref[...]
    mod = mod_ref[0]
    h = _modulated_rms_norm(x, ng_ref[...], mod[4:5], mod[3:4])
    h_scr[...] = h.astype(BF16)

    cn = FFN_BLOCK
    row = lax.broadcasted_iota(jnp.int32, (tm, cn), 0)

    def conv(cols):
        a = jnp.dot(h_scr[...], wup_ref[:, cols], preferred_element_type=F32)
        prev1 = halo_scr[1:2, cols]
        prev2 = halo_scr[0:1, cols]
        halo_scr[:, cols] = a[tm - 2:tm, :]
        a1 = jnp.where(row == 0, prev1, pltpu.roll(a, 1, 0))
        a2 = jnp.where(row == 0, prev2,
                       jnp.where(row == 1, prev1, pltpu.roll(a, 2, 0)))
        return (SQRT_HALF * cb_ref[:, cols]
                + (SQRT_HALF * cw_ref[0:1, cols]) * a2
                + (SQRT_HALF * cw_ref[1:2, cols]) * a1
                + (SQRT_HALF * cw_ref[2:3, cols]) * a)

    for j in range(hidden // cn):
        gate = conv(slice(j * cn, (j + 1) * cn))
        val = conv(slice(hidden + j * cn, hidden + (j + 1) * cn))
        p_scr[:, j * cn:(j + 1) * cn] = (
            gate * (1.0 + lax.erf(gate)) * val).astype(BF16)

    y = jnp.dot(p_scr[...], wdown_ref[...], preferred_element_type=F32)
    out = x + mod[5:6] * y
    if final_norm:
        ms = jnp.mean(out * out, axis=-1, keepdims=True)
        out = (out * lax.rsqrt(ms + EPS)) * fg_ref[...]
    o_ref[...] = out


def _ffn_layer(x2, mod, norm_g, w_up, conv_w, layer, conv_b, w_down, final_g,
               batch, seq, final_norm, next_weights):
    n, d = x2.shape
    tm = TOKEN_TILE
    tiles_per_seq = seq // tm
    hidden = w_down.shape[0]
    return _call_with_casts(
        functools.partial(_ffn_kernel, final_norm=final_norm), 8, next_weights,
        grid=(batch, tiles_per_seq),
        step_of=lambda b, t: b * tiles_per_seq + t,
        in_specs=[
            pl.BlockSpec((tm, d), lambda b, t: (b * tiles_per_seq + t, 0)),
            pl.BlockSpec((1, 6, d), lambda b, t: (b, 0, 0)),
            _resident((1, d)),
            _resident(w_up.shape),
            _resident_layer(conv_w.shape, layer),
            _resident((1, 2 * hidden)),
            _resident(w_down.shape),
            _resident((1, d)),
        ],
        out_spec=pl.BlockSpec((tm, d), lambda b, t: (b * tiles_per_seq + t, 0)),
        out_shape=jax.ShapeDtypeStruct((n, d), F32),
        inputs=(x2, mod, norm_g.reshape(1, d), w_up, conv_w,
                conv_b.reshape(1, 2 * hidden), w_down, final_g.reshape(1, d)),
        scratch_shapes=[
            pltpu.VMEM((tm, d), BF16),
            pltpu.VMEM((tm, hidden), BF16),
            pltpu.VMEM((CONV_WIDTH - 1, 2 * hidden), F32),
        ],
        compiler_params=pltpu.CompilerParams(
            dimension_semantics=("arbitrary", "arbitrary"),
            vmem_limit_bytes=VMEM_LIMIT_BYTES),
        name="conv_ffn",
    )


def kernel(x, c, gm_w_in, gm_ln_g, gm_ln_b, gm_w_s, gm_b_s, gm_w_out, hg_w_in, hg_lb, hg_gn_g, hg_w_out, ffn_w_up, ffn_conv_w, ffn_conv_b, ffn_w_down, norm_g, ada_w, ada_b, final_g):
    batch, seq, d = x.shape
    depth = ada_w.shape[0]
    assert seq % TOKEN_TILE == 0 and TOKEN_TILE % GM_BLOCK == 0

    def mixer_weights(i):
        if i % 2 == 0:
            return ((gm_w_in, i // 2), (gm_w_out, i // 2))
        return ((hg_w_in, i // 2), (hg_w_out, i // 2))

    c_pad = jnp.pad(c, ((0, 8 - batch), (0, 0)))
    mod_all, w_a, w_b = _ada_modulation(c_pad, ada_w, ada_b, mixer_weights(0))
    mod_all = mod_all[:, :batch].reshape(depth, batch, 6, d)

    head_dim = gm_w_out.shape[1] // GM_HEADS
    x2 = x.reshape(batch * seq, d)
    for i in range(depth):
        j = i // 2
        ffn_weights = ((ffn_w_up, i), (ffn_w_down, i))
        if i % 2 == 0:
            b_s_tile = jnp.repeat(gm_b_s[j].T, head_dim, axis=1)
            x2, w_a, w_b = _gmlp_layer(
                x2, mod_all[i], norm_g[i, 0], w_a, gm_ln_g[j], gm_ln_b[j], gm_w_s,
                j, b_s_tile, w_b, seq, ffn_weights)
        else:
            x2, w_a, w_b = _hgrn_layer(
                x2, mod_all[i], norm_g[i, 0], w_a, hg_lb, j, hg_gn_g[j], w_b,
                batch, seq, ffn_weights)
        last = i == depth - 1
        x2, *next_mixer = _ffn_layer(
            x2, mod_all[i], norm_g[i, 1], w_a, ffn_conv_w, i, ffn_conv_b[i], w_b,
            final_g, batch, seq, final_norm=last,
            next_weights=() if last else mixer_weights(i + 1))
        if not last:
            w_a, w_b = next_mixer
    return x2.reshape(batch, seq, d)
```

```python
import functools

import jax
import jax.numpy as jnp
from jax import lax
from jax.experimental import pallas as pl
from jax.experimental.pallas import tpu as pltpu

F32 = jnp.float32
BF16 = jnp.bfloat16

EPS = 1e-6
CHUNK = 64
GM_BLOCK = 128
GM_HEADS = 8
HG_HEADS = 8
HG_DIM = 128
SCAN_CHUNK = 64
SCAN_HALVES = (32, 16, 8, 4, 2)
CHUNKS_PER_GROUP = 1
SCAN_SCHEDULE = ("F0a S1a F1a S2a F2a S3a F3a M0a M1a B0a U0a M2a B1a U1a M3a B2a "
                 "U2a U3a B3a N0")
VREG_ROWS = 8
BF16_TILE_ROWS = 16
LOG2_E = 1.4426950408889634
CONV_WIDTH = 3
FFN_BLOCK = 256
SQRT_HALF = 0.7071067811865476

TOKEN_TILE = 512
VMEM_LIMIT_BYTES = 56 * 1024 * 1024


def _gelu(x):
    return 0.5 * x * (1.0 + lax.erf(x * SQRT_HALF))


def _silu(x):
    return x * (0.5 * jnp.tanh(0.5 * x) + 0.5)


def _modulated_rms_norm(x, gain, scale, shift):
    ms = jnp.mean(x * x, axis=-1, keepdims=True)
    return (x * lax.rsqrt(ms + EPS)) * (gain * (1.0 + scale)) + shift


def _resident(shape):
    zeros = (0,) * len(shape)
    return pl.BlockSpec(shape, lambda *_: zeros, pipeline_mode=pl.Buffered(1))


def _resident_layer(stacked_shape, layer):
    index = (layer,) + (0,) * (len(stacked_shape) - 1)
    return pl.BlockSpec((None,) + tuple(stacked_shape[1:]), lambda *_: index,
                        pipeline_mode=pl.Buffered(1))


def _cast_specs(stacked, layer, grid_steps, step_of):
    _, rows, cols = stacked.shape
    block = next(r for r in range(BF16_TILE_ROWS, rows + 1, BF16_TILE_ROWS)
                 if rows % r == 0 and r * grid_steps >= rows)
    last = rows // block - 1

    def block_index(*grid_indices):
        return jnp.minimum(step_of(*grid_indices), last)

    return (pl.BlockSpec((None, block, cols),
                         lambda *g: (layer, block_index(*g), 0)),
            pl.BlockSpec((block, cols), lambda *g: (block_index(*g), 0)),
            jax.ShapeDtypeStruct((rows, cols), BF16))


def _with_casts(body, n_inputs, n_casts):
    def kernel(*refs):
        out_at = n_inputs + n_casts
        for src, dst in zip(refs[n_inputs:out_at],
                            refs[out_at + 1:out_at + 1 + n_casts]):
            dst[...] = src[...].astype(BF16)
        body(*refs[:n_inputs], refs[out_at], *refs[out_at + 1 + n_casts:])
    return kernel


def _call_with_casts(body, n_inputs, next_weights, grid, step_of, in_specs,
                     out_spec, out_shape, inputs, **kwargs):
    steps = 1
    for extent in grid:
        steps *= extent
    casts = [_cast_specs(w, layer, steps, step_of) for w, layer in next_weights]
    return pl.pallas_call(
        _with_casts(body, n_inputs, len(casts)),
        grid=grid,
        in_specs=list(in_specs) + [c[0] for c in casts],
        out_specs=[out_spec] + [c[1] for c in casts],
        out_shape=[out_shape] + [c[2] for c in casts],
        **kwargs,
    )(*inputs, *[w for w, _ in next_weights])


def _ada_kernel(c_ref, w_ref, b_ref, o_ref):
    cond = _silu(c_ref[...])
    o_ref[0] = jnp.dot(cond, w_ref[0], preferred_element_type=F32) + b_ref[0]


def _ada_modulation(c_pad, ada_w, ada_b, next_weights):
    depth, d, n = ada_w.shape
    rows = c_pad.shape[0]
    tn = 1536
    return _call_with_casts(
        _ada_kernel, 3, next_weights,
        grid=(depth, n // tn),
        step_of=lambda l, j: l * (n // tn) + j,
        in_specs=[
            pl.BlockSpec((rows, d), lambda l, j: (0, 0)),
            pl.BlockSpec((1, d, tn), lambda l, j: (l, 0, j)),
            pl.BlockSpec((1, 1, tn), lambda l, j: (l, 0, j)),
        ],
        out_spec=pl.BlockSpec((1, rows, tn), lambda l, j: (l, 0, j)),
        out_shape=jax.ShapeDtypeStruct((depth, rows, n), F32),
        inputs=(c_pad, ada_w, ada_b.reshape(depth, 1, n)),
        compiler_params=pltpu.CompilerParams(
            dimension_semantics=("arbitrary", "arbitrary"),
            vmem_limit_bytes=VMEM_LIMIT_BYTES),
        name="ada_modulation",
    )


def _gmlp_kernel(x_ref, mod_ref, ng_ref, win_ref, lng_ref, lnb_ref, ws_ref,
                 bs_ref, wout_ref, o_ref, h_scr, u_scr, v_scr, vn_scr, gt_scr):
    tm, d = x_ref.shape
    width = u_scr.shape[1]
    head_dim = width // GM_HEADS
    x = x_ref[...]
    mod = mod_ref[0]
    h = _modulated_rms_norm(x, ng_ref[...], mod[1:2], mod[0:1])
    h_scr[...] = h.astype(BF16)

    cn = 512

    def store_gelu(dst_ref, j, z):
        dst_ref[:, j * cn:(j + 1) * cn] = _gelu(z)

    def layer_norm_v():
        v = v_scr[...]
        mu = jnp.mean(v, axis=-1, keepdims=True)
        dv = v - mu
        var = jnp.mean(dv * dv, axis=-1, keepdims=True)
        vn = dv * lax.rsqrt(var + EPS) * lng_ref[...] + lnb_ref[...]
        vn_scr[...] = vn.astype(BF16)

    row = lax.broadcasted_iota(jnp.int32, (GM_BLOCK, GM_BLOCK), 0)
    col = lax.broadcasted_iota(jnp.int32, (GM_BLOCK, GM_BLOCK), 1)
    allowed = (row >= CHUNK) | (col < CHUNK)

    def gate(hd):
        w = jnp.where(allowed, ws_ref[hd], 0.0).astype(BF16)
        cols = slice(hd * head_dim, (hd + 1) * head_dim)
        for r in range(tm // GM_BLOCK):
            rows = slice(r * GM_BLOCK, (r + 1) * GM_BLOCK)
            s = jnp.dot(w, vn_scr[rows, cols], preferred_element_type=F32)
            s = s + bs_ref[:, cols]
            gt_scr[rows, cols] = (u_scr[rows, cols] * s).astype(BF16)

    blocks = width // cn
    heads_per_block = cn // head_dim
    elementwise = []
    for dst_ref, first_col in ((v_scr, width), (u_scr, 0)):
        for j in range(blocks):
            cols = slice(first_col + j * cn, first_col + (j + 1) * cn)
            z = jnp.dot(h_scr[...], win_ref[:, cols], preferred_element_type=F32)
            for work in elementwise:
                work()
            elementwise = [functools.partial(store_gelu, dst_ref, j, z)]
            if dst_ref is v_scr and j == blocks - 1:
                elementwise.append(layer_norm_v)
            if dst_ref is u_scr:
                elementwise += [functools.partial(gate, hd) for hd in range(
                    j * heads_per_block, (j + 1) * heads_per_block)]
    for work in elementwise:
        work()
    y = jnp.dot(gt_scr[...], wout_ref[...], preferred_element_type=F32)
    o_ref[...] = x + mod[2:3] * y


def _gmlp_layer(x2, mod, norm_g, w_in, ln_g, ln_b, w_s, layer, b_s_tile, w_out,
                seq, next_weights):
    n, d = x2.shape
    tm = TOKEN_TILE
    width = w_out.shape[0]
    tiles_per_seq = seq // tm
    return _call_with_casts(
        _gmlp_kernel, 9, next_weights,
        grid=(n // tm,),
        step_of=lambda t: t,
        in_specs=[
            pl.BlockSpec((tm, d), lambda t: (t, 0)),
            pl.BlockSpec((1, 6, d), lambda t: (t // tiles_per_seq, 0, 0)),
            _resident((1, d)),
            _resident(w_in.shape),
            _resident((1, width)),
            _resident((1, width)),
            _resident_layer(w_s.shape, layer),
            _resident(b_s_tile.shape),
            _resident(w_out.shape),
        ],
        out_spec=pl.BlockSpec((tm, d), lambda t: (t, 0)),
        out_shape=jax.ShapeDtypeStruct((n, d), F32),
        inputs=(x2, mod, norm_g.reshape(1, d), w_in, ln_g.reshape(1, width),
                ln_b.reshape(1, width), w_s, b_s_tile, w_out),
        scratch_shapes=[
            pltpu.VMEM((tm, d), BF16),
            pltpu.VMEM((tm, width), F32),
            pltpu.VMEM((tm, width), F32),
            pltpu.VMEM((tm, width), BF16),
            pltpu.VMEM((tm, width), BF16),
        ],
        compiler_params=pltpu.CompilerParams(
            dimension_semantics=("arbitrary",),
            vmem_limit_bytes=VMEM_LIMIT_BYTES),
        name="gmlp_mixer",
    )


def _chunk_cumsum(g):
    groups = SCAN_CHUNK // VREG_ROWS
    g3 = g.reshape(groups, VREG_ROWS, g.shape[1])
    sub = lax.broadcasted_iota(jnp.int32, g3.shape, 1)
    step = 1
    while step < VREG_ROWS:
        g3 = g3 + jnp.where(sub >= step, pltpu.roll(g3, step, 1), 0.0)
        step *= 2
    pieces = [g3[0]]
    for r in range(1, groups):
        pieces.append(g3[r] + pieces[-1][VREG_ROWS - 1:VREG_ROWS, :])
    return jnp.concatenate(pieces, axis=0)


def _split_operands(q, k, cum, g, half):
    lanes = q.shape[1]
    if half >= VREG_ROWS:
        zeros = jnp.zeros((half, lanes), F32)
        q_parts, k_parts = [], []
        for start in range(0, SCAN_CHUNK, 2 * half):
            mid = start + half
            bnd = cum[mid - 1:mid, :]
            k_parts += [k[start:mid] * jnp.exp2(bnd - cum[start:mid]), zeros]
            q_parts += [zeros,
                        q[mid:mid + half] * jnp.exp2(cum[mid:mid + half] - bnd)]
        return jnp.concatenate(q_parts, axis=0), jnp.concatenate(k_parts, axis=0)
    groups = SCAN_CHUNK // VREG_ROWS
    cum3 = cum.reshape(groups, VREG_ROWS, lanes)
    sub = lax.broadcasted_iota(jnp.int32, cum3.shape, 1)
    bnd = None
    for start in range(0, VREG_ROWS, 2 * half):
        piece = jnp.broadcast_to(
            cum3[:, start + half - 1:start + half, :], cum3.shape)
        bnd = piece if bnd is None else jnp.where(sub >= start, piece, bnd)
    e = jnp.exp2(-jnp.abs(cum3 - bnd)).reshape(SCAN_CHUNK, lanes)
    return q * e, k * e


def _stage_head(rows, lanes, q_ref, k_ref, v_ref, g_ref, chunk, stage):
    ql_ref, ku_ref, qin_ref, kout_ref, v16_ref, direct_ref, decay_ref = stage
    q, k, v, g = (ref[rows, lanes] for ref in (q_ref, k_ref, v_ref, g_ref))
    cum = _chunk_cumsum(g)
    last = cum[SCAN_CHUNK - 1:SCAN_CHUNK, :]
    qin_ref[rows, lanes] = (q * jnp.exp2(cum)).astype(BF16)
    kout_ref[rows, lanes] = (k * jnp.exp2(last - cum)).astype(BF16)
    v16_ref[rows, lanes] = v.astype(BF16)
    decay_ref[chunk % 2, :, lanes] = jnp.broadcast_to(
        jnp.exp2(last), (HG_DIM, lanes.stop - lanes.start)).T
    groups = SCAN_CHUNK // VREG_ROWS

    def previous_row(a):
        a3 = a.reshape(groups, VREG_ROWS, a.shape[1])
        return pltpu.roll(a3, 1, 1).reshape(a.shape)

    odd = (lax.broadcasted_iota(jnp.int32, q.shape, 0) & 1) == 1
    diagonal = jnp.sum(q * k, axis=-1, keepdims=True)
    adjacent = jnp.sum(jnp.where(odd, q * previous_row(k) * jnp.exp2(g), 0.0),
                       axis=-1, keepdims=True)
    direct_ref[rows, lanes] = diagonal * v + adjacent * previous_row(v)
    for idx, half in enumerate(SCAN_HALVES):
        q_l, k_u = _split_operands(q, k, cum, g, half)
        ql_ref[idx, rows, lanes] = q_l.astype(BF16)
        ku_ref[idx, rows, lanes] = k_u.astype(BF16)


def _block_diag(a, b):
    zeros = jnp.zeros_like(a)
    return jnp.concatenate([jnp.concatenate([a, zeros], axis=1),
                            jnp.concatenate([zeros, b], axis=1)], axis=0)


def _by_key_position(a, b):
    half = SCAN_CHUNK // 2
    return jnp.concatenate([_block_diag(a[:half], b[:half]),
                            _block_diag(a[half:], b[half:])], axis=0)


def _combine_head_pair(pair, rows, chunk, level_of_pair, stage, state_ref):
    ql_ref, ku_ref, qin_ref, kout_ref, v16_ref, direct_ref, decay_ref = stage
    nt = (((1,), (1,)), ((), ()))
    first, second = 2 * pair, 2 * pair + 1
    lanes_a = slice(first * HG_DIM, second * HG_DIM)
    lanes_b = slice(second * HG_DIM, (second + 1) * HG_DIM)
    lanes = slice(first * HG_DIM, (second + 1) * HG_DIM)

    def level_scores(idx):
        keys = _by_key_position(ku_ref[idx, rows, lanes_a],
                                ku_ref[idx, rows, lanes_b])
        if idx == 0:
            keys = keys[:SCAN_CHUNK]
        s = lax.dot_general(ql_ref[idx, rows, lanes], keys, nt,
                            preferred_element_type=F32)
        return jnp.concatenate([s, jnp.zeros_like(s)], axis=1) if idx == 0 else s

    level_scores = [level_scores(idx) for idx in range(len(SCAN_HALVES))]
    state_a, state_b = state_ref[first], state_ref[second]
    out = direct_ref[rows, lanes] + jnp.dot(
        qin_ref[rows, lanes],
        _block_diag(state_a.astype(BF16), state_b.astype(BF16)),
        preferred_element_type=F32)
    v_bd = _by_key_position(v16_ref[rows, lanes_a], v16_ref[rows, lanes_b])

    def advance_state():
        update = lax.dot_general(
            _by_key_position(kout_ref[rows, lanes_a], kout_ref[rows, lanes_b]), v_bd,
            (((0,), (0,)), ((), ())), preferred_element_type=F32)
        state_ref[first] = (state_a * decay_ref[chunk % 2, :, lanes_a]
                            + update[:HG_DIM, :HG_DIM])
        state_ref[second] = (state_b * decay_ref[chunk % 2, :, lanes_b]
                             + update[HG_DIM:, HG_DIM:])

    def finish():
        scores = level_scores[0]
        for half, s in zip(SCAN_HALVES[1:], level_scores[1:]):
            scores = jnp.where(level_of_pair[half], s, scores)
        weighted = jnp.dot(scores.astype(BF16), v_bd, preferred_element_type=F32)
        return lambda: out + weighted

    return finish, advance_state


def _hgrn_lower_bound(lb_raw, layer):
    e = jnp.exp(lb_raw - jnp.max(lb_raw, axis=0, keepdims=True))
    total = jnp.sum(e, axis=0, keepdims=True)
    below = jnp.zeros_like(total)
    for m in range(1, layer + 1):
        below = below + e[m:m + 1]
    return below / total


def _hgrn_kernel(x_ref, mod_ref, ng_ref, win_ref, lb_ref, gn_ref, wout_ref,
                 o_ref, h_scr, q_scr, k_scr, v_scr, lf_scr, sg_scr, o_scr,
                 state_scr, ql_scr, ku_scr, qin_scr, kout_scr, v16_scr,
                 decay_scr, *, layer):
    tm, d = x_ref.shape
    stage = (ql_scr, ku_scr, qin_scr, kout_scr, v16_scr, o_scr, decay_scr)

    @pl.when(pl.program_id(1) == 0)
    def _():
        state_scr[...] = jnp.zeros_like(state_scr)

    x = x_ref[...]
    mod = mod_ref[0]
    h = _modulated_rms_norm(x, ng_ref[...], mod[1:2], mod[0:1])
    h_scr[...] = h.astype(BF16)

    def proj(i):
        return jnp.dot(h_scr[...], win_ref[:, i * d:(i + 1) * d],
                       preferred_element_type=F32)

    lb = _hgrn_lower_bound(lb_ref[...], layer)
    q_lin = proj(0)
    f_lin = proj(1)
    q_scr[...] = _silu(q_lin)
    v_scr[...] = proj(2)
    f = lb + (1.0 - lb) * jax.nn.sigmoid(f_lin)
    lf_scr[...] = jnp.log(f) * LOG2_E
    k_scr[...] = 1.0 - f
    sg_scr[...] = _silu(proj(3))

    head_lanes = [slice(hd * HG_DIM, (hd + 1) * HG_DIM)
                  for hd in range(HG_HEADS)]

    def chunk_rows(c):
        start = c * SCAN_CHUNK
        if not isinstance(c, int):
            start = pl.multiple_of(start, SCAN_CHUNK)
        return pl.ds(start, SCAN_CHUNK)

    def normalize_pair(rows, pair, out):
        for hd in (2 * pair, 2 * pair + 1):
            o = out[:, (hd % 2) * HG_DIM:(hd % 2 + 1) * HG_DIM]
            ms = jnp.mean(o * o, axis=-1, keepdims=True)
            o_scr[rows, head_lanes[hd]] = o * lax.rsqrt(ms + EPS)

    def stage_pair(c, pair):
        for hd in (2 * pair, 2 * pair + 1):
            _stage_head(chunk_rows(c), head_lanes[hd], q_scr, k_scr, v_scr,
                        lf_scr, c, stage)

    def chunk_group_body(group, carry, schedule):
        first_chunk = group * CHUNKS_PER_GROUP
        half_chunk = SCAN_CHUNK // 2
        row_s = lax.broadcasted_iota(jnp.int32, (SCAN_CHUNK, 2 * SCAN_CHUNK), 0)
        lane_s = lax.broadcasted_iota(jnp.int32, (SCAN_CHUNK, 2 * SCAN_CHUNK), 1)
        col_s = ((lane_s >> (SCAN_CHUNK.bit_length() - 1)) * half_chunk
                 + (lane_s & (half_chunk - 1)))
        level_of_pair = {
            half: (((row_s ^ col_s) >> (half.bit_length() - 1)) == 1)
            & (row_s > col_s) for half in SCAN_HALVES[1:]}
        finish, advance_state, emit = {}, {}, {}
        for step in schedule.split():
            kind, pair = step[0], int(step[1])
            if kind == "N":
                stage_pair(first_chunk + CHUNKS_PER_GROUP, pair)
                continue
            c = first_chunk + "ab".index(step[2])
            item = step[1:]
            if kind == "S":
                stage_pair(c, pair)
            elif kind == "F":
                finish[item], advance_state[item] = _combine_head_pair(
                    pair, chunk_rows(c), c, level_of_pair, stage, state_scr)
            elif kind == "M":
                emit[item] = finish[item]()
            elif kind == "B":
                normalize_pair(chunk_rows(c), pair, emit[item]())
            else:
                advance_state[item]()
        return carry

    groups = tm // (SCAN_CHUNK * CHUNKS_PER_GROUP)
    stage_pair(0, 0)
    lax.fori_loop(0, groups - 1,
                  functools.partial(chunk_group_body, schedule=SCAN_SCHEDULE), 0)
    chunk_group_body(groups - 1, 0, SCAN_SCHEDULE.replace(" N0", ""))

    gated = (o_scr[...] * gn_ref[...]) * sg_scr[...]
    y = jnp.dot(gated.astype(BF16), wout_ref[...], preferred_element_type=F32)
    o_ref[...] = x + mod[2:3] * y


def _hgrn_layer(x2, mod, norm_g, w_in, lb_raw, layer, gn_g, w_out, batch, seq,
                next_weights):
    n, d = x2.shape
    tm = TOKEN_TILE
    tiles_per_seq = seq // tm
    act = pltpu.VMEM((tm, d), F32)
    return _call_with_casts(
        functools.partial(_hgrn_kernel, layer=layer), 7, next_weights,
        grid=(batch, tiles_per_seq),
        step_of=lambda b, t: b * tiles_per_seq + t,
        in_specs=[
            pl.BlockSpec((tm, d), lambda b, t: (b * tiles_per_seq + t, 0)),
            pl.BlockSpec((1, 6, d), lambda b, t: (b, 0, 0)),
            _resident((1, d)),
            _resident(w_in.shape),
            _resident(lb_raw.shape),
            _resident((1, d)),
            _resident(w_out.shape),
        ],
        out_spec=pl.BlockSpec((tm, d), lambda b, t: (b * tiles_per_seq + t, 0)),
        out_shape=jax.ShapeDtypeStruct((n, d), F32),
        inputs=(x2, mod, norm_g.reshape(1, d), w_in, lb_raw.astype(F32),
                gn_g.reshape(1, d), w_out),
        scratch_shapes=[
            pltpu.VMEM((tm, d), BF16),
            act, act, act, act, act, act,
            pltpu.VMEM((HG_HEADS, HG_DIM, HG_DIM), F32),
            pltpu.VMEM((len(SCAN_HALVES), tm, d), BF16),
            pltpu.VMEM((len(SCAN_HALVES), tm, d), BF16),
            pltpu.VMEM((tm, d), BF16),
            pltpu.VMEM((tm, d), BF16),
            pltpu.VMEM((tm, d), BF16),
            pltpu.VMEM((2, HG_DIM, d), F32),
        ],
        compiler_params=pltpu.CompilerParams(
            dimension_semantics=("arbitrary", "arbitrary"),
            vmem_limit_bytes=VMEM_LIMIT_BYTES),
        name="hgrn_mixer",
    )


def _ffn_kernel(x_ref, mod_ref, ng_ref, wup_ref, cw_ref, cb_ref, wdown_ref,
                fg_ref, o_ref, h_scr, p_scr, halo_scr, *, final_norm):
    tm, d = x_ref.shape
    hidden = wdown_ref.shape[0]

    @pl.when(pl.program_id(1) == 0)
    def _():
        halo_scr[...] = jnp.zeros_like(halo_scr)

    x = x_ref[...]
    mod = mod_ref[0]
    h = _modulated_rms_norm(x, ng_ref[...], mod[4:5], mod[3:4])
    h_scr[...] = h.astype(BF16)

    cn = FFN_BLOCK
    row = lax.broadcasted_iota(jnp.int32, (tm, cn), 0)

    def conv(cols):
        a = jnp.dot(h_scr[...], wup_ref[:, cols], preferred_element_type=F32)
        prev1 = halo_scr[1:2, cols]
        prev2 = halo_scr[0:1, cols]
        halo_scr[:, cols] = a[tm - 2:tm, :]
        a1 = jnp.where(row == 0, prev1, pltpu.roll(a, 1, 0))
        a2 = jnp.where(row == 0, prev2,
                       jnp.where(row == 1, prev1, pltpu.roll(a, 2, 0)))
        return (SQRT_HALF * cb_ref[:, cols]
                + (SQRT_HALF * cw_ref[0:1, cols]) * a2
                + (SQRT_HALF * cw_ref[1:2, cols]) * a1
                + (SQRT_HALF * cw_ref[2:3, cols]) * a)

    for j in range(hidden // cn):
        gate = conv(slice(j * cn, (j + 1) * cn))
        val = conv(slice(hidden + j * cn, hidden + (j + 1) * cn))
        p_scr[:, j * cn:(j + 1) * cn] = (
            gate * (1.0 + lax.erf(gate)) * val).astype(BF16)

    y = jnp.dot(p_scr[...], wdown_ref[...], preferred_element_type=F32)
    out = x + mod[5:6] * y
    if final_norm:
        ms = jnp.mean(out * out, axis=-1, keepdims=True)
        out = (out * lax.rsqrt(ms + EPS)) * fg_ref[...]
    o_ref[...] = out


def _ffn_layer(x2, mod, norm_g, w_up, conv_w, layer, conv_b, w_down, final_g,
               batch, seq, final_norm, next_weights):
    n, d = x2.shape
    tm = TOKEN_TILE
    tiles_per_seq = seq // tm
    hidden = w_down.shape[0]
    return _call_with_casts(
        functools.partial(_ffn_kernel, final_norm=final_norm), 8, next_weights,
        grid=(batch, tiles_per_seq),
        step_of=lambda b, t: b * tiles_per_seq + t,
        in_specs=[
            pl.BlockSpec((tm, d), lambda b, t: (b * tiles_per_seq + t, 0)),
            pl.BlockSpec((1, 6, d), lambda b, t: (b, 0, 0)),
            _resident((1, d)),
            _resident(w_up.shape),
            _resident_layer(conv_w.shape, layer),
            _resident((1, 2 * hidden)),
            _resident(w_down.shape),
            _resident((1, d)),
        ],
        out_spec=pl.BlockSpec((tm, d), lambda b, t: (b * tiles_per_seq + t, 0)),
        out_shape=jax.ShapeDtypeStruct((n, d), F32),
        inputs=(x2, mod, norm_g.reshape(1, d), w_up, conv_w,
                conv_b.reshape(1, 2 * hidden), w_down, final_g.reshape(1, d)),
        scratch_shapes=[
            pltpu.VMEM((tm, d), BF16),
            pltpu.VMEM((tm, hidden), BF16),
            pltpu.VMEM((CONV_WIDTH - 1, 2 * hidden), F32),
        ],
        compiler_params=pltpu.CompilerParams(
            dimension_semantics=("arbitrary", "arbitrary"),
            vmem_limit_bytes=VMEM_LIMIT_BYTES),
        name="conv_ffn",
    )


def kernel(x, c, gm_w_in, gm_ln_g, gm_ln_b, gm_w_s, gm_b_s, gm_w_out, hg_w_in, hg_lb, hg_gn_g, hg_w_out, ffn_w_up, ffn_conv_w, ffn_conv_b, ffn_w_down, norm_g, ada_w, ada_b, final_g):
    batch, seq, d = x.shape
    depth = ada_w.shape[0]
    assert seq % TOKEN_TILE == 0 and TOKEN_TILE % GM_BLOCK == 0

    def mixer_weights(i):
        if i % 2 == 0:
            return ((gm_w_in, i // 2), (gm_w_out, i // 2))
        return ((hg_w_in, i // 2), (hg_w_out, i // 2))

    c_pad = jnp.pad(c, ((0, 8 - batch), (0, 0)))
    mod_all, w_a, w_b = _ada_modulation(c_pad, ada_w, ada_b, mixer_weights(0))
    mod_all = mod_all[:, :batch].reshape(depth, batch, 6, d)

    head_dim = gm_w_out.shape[1] // GM_HEADS
    x2 = x.reshape(batch * seq, d)
    for i in range(depth):
        j = i // 2
        ffn_weights = ((ffn_w_up, i), (ffn_w_down, i))
        if i % 2 == 0:
            b_s_tile = jnp.repeat(gm_b_s[j].T, head_dim, axis=1)
            x2, w_a, w_b = _gmlp_layer(
                x2, mod_all[i], norm_g[i, 0], w_a, gm_ln_g[j], gm_ln_b[j], gm_w_s,
                j, b_s_tile, w_b, seq, ffn_weights)
        else:
            x2, w_a, w_b = _hgrn_layer(
                x2, mod_all[i], norm_g[i, 0], w_a, hg_lb, j, hg_gn_g[j], w_b,
                batch, seq, ffn_weights)
        last = i == depth - 1
        x2, *next_mixer = _ffn_layer(
            x2, mod_all[i], norm_g[i, 1], w_a, ffn_conv_w, i, ffn_conv_b[i], w_b,
            final_g, batch, seq, final_norm=last,
            next_weights=() if last else mixer_weights(i + 1))
        if not last:
            w_a, w_b = next_mixer
    return x2.reshape(batch, seq, d)
```

```python
import functools

import jax
import jax.numpy as jnp
from jax import lax
from jax.experimental import pallas as pl
from jax.experimental.pallas import tpu as pltpu

F32 = jnp.float32
BF16 = jnp.bfloat16

EPS = 1e-6
CHUNK = 64
GM_BLOCK = 128
GM_HEADS = 8
HG_HEADS = 8
HG_DIM = 128
SCAN_CHUNK = 64
SCAN_HALVES = (32, 16, 8, 4, 2)
CHUNKS_PER_GROUP = 1
SCAN_SCHEDULE = ("F0a S1a F1a S2a F2a S3a F3a M0a M1a M2a B0a M3a B1a B2a U0a U1a "
                 "B3a U2a U3a N0")
VREG_ROWS = 8
BF16_TILE_ROWS = 16
LOG2_E = 1.4426950408889634
CONV_WIDTH = 3
FFN_BLOCK = 256
SQRT_HALF = 0.7071067811865476

TOKEN_TILE = 512
VMEM_LIMIT_BYTES = 56 * 1024 * 1024


def _gelu(x):
    return 0.5 * x * (1.0 + lax.erf(x * SQRT_HALF))


def _silu(x):
    return x * (0.5 * jnp.tanh(0.5 * x) + 0.5)


def _modulated_rms_norm(x, gain, scale, shift):
    ms = jnp.mean(x * x, axis=-1, keepdims=True)
    return (x * lax.rsqrt(ms + EPS)) * (gain * (1.0 + scale)) + shift


def _resident(shape):
    zeros = (0,) * len(shape)
    return pl.BlockSpec(shape, lambda *_: zeros, pipeline_mode=pl.Buffered(1))


def _resident_layer(stacked_shape, layer):
    index = (layer,) + (0,) * (len(stacked_shape) - 1)
    return pl.BlockSpec((None,) + tuple(stacked_shape[1:]), lambda *_: index,
                        pipeline_mode=pl.Buffered(1))


def _cast_specs(stacked, layer, grid_steps, step_of):
    _, rows, cols = stacked.shape
    block = next(r for r in range(BF16_TILE_ROWS, rows + 1, BF16_TILE_ROWS)
                 if rows % r == 0 and r * grid_steps >= rows)
    last = rows // block - 1

    def block_index(*grid_indices):
        return jnp.minimum(step_of(*grid_indices), last)

    return (pl.BlockSpec((None, block, cols),
                         lambda *g: (layer, block_index(*g), 0)),
            pl.BlockSpec((block, cols), lambda *g: (block_index(*g), 0)),
            jax.ShapeDtypeStruct((rows, cols), BF16))


def _with_casts(body, n_inputs, n_casts):
    def kernel(*refs):
        out_at = n_inputs + n_casts
        for src, dst in zip(refs[n_inputs:out_at],
                            refs[out_at + 1:out_at + 1 + n_casts]):
            dst[...] = src[...].astype(BF16)
        body(*refs[:n_inputs], refs[out_at], *refs[out_at + 1 + n_casts:])
    return kernel


def _call_with_casts(body, n_inputs, next_weights, grid, step_of, in_specs,
                     out_spec, out_shape, inputs, **kwargs):
    steps = 1
    for extent in grid:
        steps *= extent
    casts = [_cast_specs(w, layer, steps, step_of) for w, layer in next_weights]
    return pl.pallas_call(
        _with_casts(body, n_inputs, len(casts)),
        grid=grid,
        in_specs=list(in_specs) + [c[0] for c in casts],
        out_specs=[out_spec] + [c[1] for c in casts],
        out_shape=[out_shape] + [c[2] for c in casts],
        **kwargs,
    )(*inputs, *[w for w, _ in next_weights])


def _ada_kernel(c_ref, w_ref, b_ref, o_ref):
    cond = _silu(c_ref[...])
    o_ref[0] = jnp.dot(cond, w_ref[0], preferred_element_type=F32) + b_ref[0]


def _ada_modulation(c_pad, ada_w, ada_b, next_weights):
    depth, d, n = ada_w.shape
    rows = c_pad.shape[0]
    tn = 1536
    return _call_with_casts(
        _ada_kernel, 3, next_weights,
        grid=(depth, n // tn),
        step_of=lambda l, j: l * (n // tn) + j,
        in_specs=[
            pl.BlockSpec((rows, d), lambda l, j: (0, 0)),
            pl.BlockSpec((1, d, tn), lambda l, j: (l, 0, j)),
            pl.BlockSpec((1, 1, tn), lambda l, j: (l, 0, j)),
        ],
        out_spec=pl.BlockSpec((1, rows, tn), lambda l, j: (l, 0, j)),
        out_shape=jax.ShapeDtypeStruct((depth, rows, n), F32),
        inputs=(c_pad, ada_w, ada_b.reshape(depth, 1, n)),
        compiler_params=pltpu.CompilerParams(
            dimension_semantics=("arbitrary", "arbitrary"),
            vmem_limit_bytes=VMEM_LIMIT_BYTES),
        name="ada_modulation",
    )


def _gmlp_kernel(x_ref, mod_ref, ng_ref, win_ref, lng_ref, lnb_ref, ws_ref,
                 bs_ref, wout_ref, o_ref, h_scr, u_scr, v_scr, vn_scr, gt_scr):
    tm, d = x_ref.shape
    width = u_scr.shape[1]
    head_dim = width // GM_HEADS
    x = x_ref[...]
    mod = mod_ref[0]
    h = _modulated_rms_norm(x, ng_ref[...], mod[1:2], mod[0:1])
    h_scr[...] = h.astype(BF16)

    cn = 512

    def store_gelu(dst_ref, j, z):
        dst_ref[:, j * cn:(j + 1) * cn] = _gelu(z)

    def layer_norm_v():
        v = v_scr[...]
        mu = jnp.mean(v, axis=-1, keepdims=True)
        dv = v - mu
        var = jnp.mean(dv * dv, axis=-1, keepdims=True)
        vn = dv * lax.rsqrt(var + EPS) * lng_ref[...] + lnb_ref[...]
        vn_scr[...] = vn.astype(BF16)

    row = lax.broadcasted_iota(jnp.int32, (GM_BLOCK, GM_BLOCK), 0)
    col = lax.broadcasted_iota(jnp.int32, (GM_BLOCK, GM_BLOCK), 1)
    allowed = (row >= CHUNK) | (col < CHUNK)

    def gate(hd):
        w = jnp.where(allowed, ws_ref[hd], 0.0).astype(BF16)
        cols = slice(hd * head_dim, (hd + 1) * head_dim)
        for r in range(tm // GM_BLOCK):
            rows = slice(r * GM_BLOCK, (r + 1) * GM_BLOCK)
            s = jnp.dot(w, vn_scr[rows, cols], preferred_element_type=F32)
            s = s + bs_ref[:, cols]
            gt_scr[rows, cols] = (u_scr[rows, cols] * s).astype(BF16)

    blocks = width // cn
    heads_per_block = cn // head_dim
    elementwise = []
    for dst_ref, first_col in ((v_scr, width), (u_scr, 0)):
        for j in range(blocks):
            cols = slice(first_col + j * cn, first_col + (j + 1) * cn)
            z = jnp.dot(h_scr[...], win_ref[:, cols], preferred_element_type=F32)
            for work in elementwise:
                work()
            elementwise = [functools.partial(store_gelu, dst_ref, j, z)]
            if dst_ref is v_scr and j == blocks - 1:
                elementwise.append(layer_norm_v)
            if dst_ref is u_scr:
                elementwise += [functools.partial(gate, hd) for hd in range(
                    j * heads_per_block, (j + 1) * heads_per_block)]
    for work in elementwise:
        work()
    y = jnp.dot(gt_scr[...], wout_ref[...], preferred_element_type=F32)
    o_ref[...] = x + mod[2:3] * y


def _gmlp_layer(x2, mod, norm_g, w_in, ln_g, ln_b, w_s, layer, b_s_tile, w_out,
                seq, next_weights):
    n, d = x2.shape
    tm = TOKEN_TILE
    width = w_out.shape[0]
    tiles_per_seq = seq // tm
    return _call_with_casts(
        _gmlp_kernel, 9, next_weights,
        grid=(n // tm,),
        step_of=lambda t: t,
        in_specs=[
            pl.BlockSpec((tm, d), lambda t: (t, 0)),
            pl.BlockSpec((1, 6, d), lambda t: (t // tiles_per_seq, 0, 0)),
            _resident((1, d)),
            _resident(w_in.shape),
            _resident((1, width)),
            _resident((1, width)),
            _resident_layer(w_s.shape, layer),
            _resident(b_s_tile.shape),
            _resident(w_out.shape),
        ],
        out_spec=pl.BlockSpec((tm, d), lambda t: (t, 0)),
        out_shape=jax.ShapeDtypeStruct((n, d), F32),
        inputs=(x2, mod, norm_g.reshape(1, d), w_in, ln_g.reshape(1, width),
                ln_b.reshape(1, width), w_s, b_s_tile, w_out),
        scratch_shapes=[
            pltpu.VMEM((tm, d), BF16),
            pltpu.VMEM((tm, width), F32),
            pltpu.VMEM((tm, width), F32),
            pltpu.VMEM((tm, width), BF16),
            pltpu.VMEM((tm, width), BF16),
        ],
        compiler_params=pltpu.CompilerParams(
            dimension_semantics=("arbitrary",),
            vmem_limit_bytes=VMEM_LIMIT_BYTES),
        name="gmlp_mixer",
    )


def _chunk_cumsum(g):
    groups = SCAN_CHUNK // VREG_ROWS
    g3 = g.reshape(groups, VREG_ROWS, g.shape[1])
    sub = lax.broadcasted_iota(jnp.int32, g3.shape, 1)
    step = 1
    while step < VREG_ROWS:
        g3 = g3 + jnp.where(sub >= step, pltpu.roll(g3, step, 1), 0.0)
        step *= 2
    pieces = [g3[0]]
    for r in range(1, groups):
        pieces.append(g3[r] + pieces[-1][VREG_ROWS - 1:VREG_ROWS, :])
    return jnp.concatenate(pieces, axis=0)


def _split_operands(q, k, cum, g, half):
    lanes = q.shape[1]
    if half >= VREG_ROWS:
        zeros = jnp.zeros((half, lanes), F32)
        q_parts, k_parts = [], []
        for start in range(0, SCAN_CHUNK, 2 * half):
            mid = start + half
            bnd = cum[mid - 1:mid, :]
            k_parts += [k[start:mid] * jnp.exp2(bnd - cum[start:mid]), zeros]
            q_parts += [zeros,
                        q[mid:mid + half] * jnp.exp2(cum[mid:mid + half] - bnd)]
        return jnp.concatenate(q_parts, axis=0), jnp.concatenate(k_parts, axis=0)
    groups = SCAN_CHUNK // VREG_ROWS
    cum3 = cum.reshape(groups, VREG_ROWS, lanes)
    sub = lax.broadcasted_iota(jnp.int32, cum3.shape, 1)
    bnd = None
    for start in range(0, VREG_ROWS, 2 * half):
        piece = jnp.broadcast_to(
            cum3[:, start + half - 1:start + half, :], cum3.shape)
        bnd = piece if bnd is None else jnp.where(sub >= start, piece, bnd)
    e = jnp.exp2(-jnp.abs(cum3 - bnd)).reshape(SCAN_CHUNK, lanes)
    return q * e, k * e


def _stage_head(rows, lanes, q_ref, k_ref, v_ref, g_ref, chunk, stage):
    ql_ref, ku_ref, qin_ref, kout_ref, v16_ref, direct_ref, decay_ref = stage
    q, k, v, g = (ref[rows, lanes] for ref in (q_ref, k_ref, v_ref, g_ref))
    cum = _chunk_cumsum(g)
    last = cum[SCAN_CHUNK - 1:SCAN_CHUNK, :]
    qin_ref[rows, lanes] = (q * jnp.exp2(cum)).astype(BF16)
    kout_ref[rows, lanes] = (k * jnp.exp2(last - cum)).astype(BF16)
    v16_ref[rows, lanes] = v.astype(BF16)
    decay_ref[chunk, :, lanes] = jnp.exp2(last)
    groups = SCAN_CHUNK // VREG_ROWS

    def previous_row(a):
        a3 = a.reshape(groups, VREG_ROWS, a.shape[1])
        return pltpu.roll(a3, 1, 1).reshape(a.shape)

    odd = (lax.broadcasted_iota(jnp.int32, q.shape, 0) & 1) == 1
    diagonal = jnp.sum(q * k, axis=-1, keepdims=True)
    adjacent = jnp.sum(jnp.where(odd, q * previous_row(k) * jnp.exp2(g), 0.0),
                       axis=-1, keepdims=True)
    direct_ref[rows, lanes] = diagonal * v + adjacent * previous_row(v)
    for idx, half in enumerate(SCAN_HALVES):
        q_l, k_u = _split_operands(q, k, cum, g, half)
        ql_ref[idx, rows, lanes] = q_l.astype(BF16)
        ku_ref[idx, rows, lanes] = k_u.astype(BF16)


def _block_diag(a, b):
    zeros = jnp.zeros_like(a)
    return jnp.concatenate([jnp.concatenate([a, zeros], axis=1),
                            jnp.concatenate([zeros, b], axis=1)], axis=0)


def _by_key_position(a, b):
    half = SCAN_CHUNK // 2
    return jnp.concatenate([_block_diag(a[:half], b[:half]),
                            _block_diag(a[half:], b[half:])], axis=0)


def _combine_head_pair(pair, rows, chunk, level_of_pair, stage, state_ref):
    ql_ref, ku_ref, qin_ref, kout_ref, v16_ref, direct_ref, decay_ref = stage
    nt = (((1,), (1,)), ((), ()))
    first, second = 2 * pair, 2 * pair + 1
    lanes_a = slice(first * HG_DIM, second * HG_DIM)
    lanes_b = slice(second * HG_DIM, (second + 1) * HG_DIM)
    lanes = slice(first * HG_DIM, (second + 1) * HG_DIM)

    def level_scores(idx):
        keys = _by_key_position(ku_ref[idx, rows, lanes_a],
                                ku_ref[idx, rows, lanes_b])
        if idx == 0:
            keys = keys[:SCAN_CHUNK]
        s = lax.dot_general(ql_ref[idx, rows, lanes], keys, nt,
                            preferred_element_type=F32)
        return jnp.concatenate([s, jnp.zeros_like(s)], axis=1) if idx == 0 else s

    level_scores = [level_scores(idx) for idx in range(len(SCAN_HALVES))]
    state_a, state_b = state_ref[first], state_ref[second]
    out = direct_ref[rows, lanes] + lax.dot_general(
        qin_ref[rows, lanes],
        _block_diag(state_a.astype(BF16), state_b.astype(BF16)), nt,
        preferred_element_type=F32)
    v_bd = _by_key_position(v16_ref[rows, lanes_a], v16_ref[rows, lanes_b])

    def advance_state():
        update = lax.dot_general(
            v_bd, _by_key_position(kout_ref[rows, lanes_a], kout_ref[rows, lanes_b]),
            (((0,), (0,)), ((), ())), preferred_element_type=F32)
        state_ref[first] = (state_a * decay_ref[chunk, :, lanes_a]
                            + update[:HG_DIM, :HG_DIM])
        state_ref[second] = (state_b * decay_ref[chunk, :, lanes_b]
                             + update[HG_DIM:, HG_DIM:])

    def finish():
        scores = level_scores[0]
        for half, s in zip(SCAN_HALVES[1:], level_scores[1:]):
            scores = jnp.where(level_of_pair[half], s, scores)
        weighted = jnp.dot(scores.astype(BF16), v_bd, preferred_element_type=F32)
        return lambda: out + weighted

    return finish, advance_state


def _hgrn_lower_bound(lb_raw, layer):
    e = jnp.exp(lb_raw - jnp.max(lb_raw, axis=0, keepdims=True))
    total = jnp.sum(e, axis=0, keepdims=True)
    below = jnp.zeros_like(total)
    for m in range(1, layer + 1):
        below = below + e[m:m + 1]
    return below / total


def _hgrn_kernel(x_ref, mod_ref, ng_ref, win_ref, lb_ref, gn_ref, wout_ref,
                 o_ref, h_scr, q_scr, k_scr, v_scr, lf_scr, sg_scr, o_scr,
                 state_scr, ql_scr, ku_scr, qin_scr, kout_scr, v16_scr,
                 decay_scr, *, layer):
    tm, d = x_ref.shape
    stage = (ql_scr, ku_scr, qin_scr, kout_scr, v16_scr, o_scr, decay_scr)

    @pl.when(pl.program_id(1) == 0)
    def _():
        state_scr[...] = jnp.zeros_like(state_scr)

    x = x_ref[...]
    mod = mod_ref[0]
    h = _modulated_rms_norm(x, ng_ref[...], mod[1:2], mod[0:1])
    h_scr[...] = h.astype(BF16)

    def proj(i):
        return jnp.dot(h_scr[...], win_ref[:, i * d:(i + 1) * d],
                       preferred_element_type=F32)

    lb = _hgrn_lower_bound(lb_ref[...], layer)
    q_lin = proj(0)
    f_lin = proj(1)
    q_scr[...] = _silu(q_lin)
    v_scr[...] = proj(2)
    f = lb + (1.0 - lb) * jax.nn.sigmoid(f_lin)
    lf_scr[...] = jnp.log(f) * LOG2_E
    k_scr[...] = 1.0 - f
    sg_scr[...] = _silu(proj(3))

    head_lanes = [slice(hd * HG_DIM, (hd + 1) * HG_DIM)
                  for hd in range(HG_HEADS)]

    def chunk_rows(c):
        start = c * SCAN_CHUNK
        if not isinstance(c, int):
            start = pl.multiple_of(start, SCAN_CHUNK)
        return pl.ds(start, SCAN_CHUNK)

    def normalize_pair(rows, pair, out):
        for hd in (2 * pair, 2 * pair + 1):
            o = out[:, (hd % 2) * HG_DIM:(hd % 2 + 1) * HG_DIM]
            ms = jnp.mean(o * o, axis=-1, keepdims=True)
            o_scr[rows, head_lanes[hd]] = o * lax.rsqrt(ms + EPS)

    def stage_pair(c, pair):
        for hd in (2 * pair, 2 * pair + 1):
            _stage_head(chunk_rows(c), head_lanes[hd], q_scr, k_scr, v_scr,
                        lf_scr, c, stage)

    def chunk_group_body(group, carry, schedule):
        first_chunk = group * CHUNKS_PER_GROUP
        half_chunk = SCAN_CHUNK // 2
        row_s = lax.broadcasted_iota(jnp.int32, (SCAN_CHUNK, 2 * SCAN_CHUNK), 0)
        lane_s = lax.broadcasted_iota(jnp.int32, (SCAN_CHUNK, 2 * SCAN_CHUNK), 1)
        col_s = ((lane_s >> (SCAN_CHUNK.bit_length() - 1)) * half_chunk
                 + (lane_s & (half_chunk - 1)))
        level_of_pair = {
            half: (((row_s ^ col_s) >> (half.bit_length() - 1)) == 1)
            & (row_s > col_s) for half in SCAN_HALVES[1:]}
        finish, advance_state, emit = {}, {}, {}
        for step in schedule.split():
            kind, pair = step[0], int(step[1])
            if kind == "N":
                stage_pair(first_chunk + CHUNKS_PER_GROUP, pair)
                continue
            c = first_chunk + "ab".index(step[2])
            item = step[1:]
            if kind == "S":
                stage_pair(c, pair)
            elif kind == "F":
                finish[item], advance_state[item] = _combine_head_pair(
                    pair, chunk_rows(c), c, level_of_pair, stage, state_scr)
            elif kind == "M":
                emit[item] = finish[item]()
            elif kind == "B":
                normalize_pair(chunk_rows(c), pair, emit[item]())
            else:
                advance_state[item]()
        return carry

    groups = tm // (SCAN_CHUNK * CHUNKS_PER_GROUP)
    stage_pair(0, 0)
    lax.fori_loop(0, groups - 1,
                  functools.partial(chunk_group_body, schedule=SCAN_SCHEDULE), 0)
    chunk_group_body(groups - 1, 0, SCAN_SCHEDULE.replace(" N0", ""))

    gated = (o_scr[...] * gn_ref[...]) * sg_scr[...]
    y = jnp.dot(gated.astype(BF16), wout_ref[...], preferred_element_type=F32)
    o_ref[...] = x + mod[2:3] * y


def _hgrn_layer(x2, mod, norm_g, w_in, lb_raw, layer, gn_g, w_out, batch, seq,
                next_weights):
    n, d = x2.shape
    tm = TOKEN_TILE
    tiles_per_seq = seq // tm
    act = pltpu.VMEM((tm, d), F32)
    return _call_with_casts(
        functools.partial(_hgrn_kernel, layer=layer), 7, next_weights,
        grid=(batch, tiles_per_seq),
        step_of=lambda b, t: b * tiles_per_seq + t,
        in_specs=[
            pl.BlockSpec((tm, d), lambda b, t: (b * tiles_per_seq + t, 0)),
            pl.BlockSpec((1, 6, d), lambda b, t: (b, 0, 0)),
            _resident((1, d)),
            _resident(w_in.shape),
            _resident(lb_raw.shape),
            _resident((1, d)),
            _resident(w_out.shape),
        ],
        out_spec=pl.BlockSpec((tm, d), lambda b, t: (b * tiles_per_seq + t, 0)),
        out_shape=jax.ShapeDtypeStruct((n, d), F32),
        inputs=(x2, mod, norm_g.reshape(1, d), w_in, lb_raw.astype(F32),
                gn_g.reshape(1, d), w_out),
        scratch_shapes=[
            pltpu.VMEM((tm, d), BF16),
            act, act, act, act, act, act,
            pltpu.VMEM((HG_HEADS, HG_DIM, HG_DIM), F32),
            pltpu.VMEM((len(SCAN_HALVES), tm, d), BF16),
            pltpu.VMEM((len(SCAN_HALVES), tm, d), BF16),
            pltpu.VMEM((tm, d), BF16),
            pltpu.VMEM((tm, d), BF16),
            pltpu.VMEM((tm, d), BF16),
            pltpu.VMEM((tm // SCAN_CHUNK, 1, d), F32),
        ],
        compiler_params=pltpu.CompilerParams(
            dimension_semantics=("arbitrary", "arbitrary"),
            vmem_limit_bytes=VMEM_LIMIT_BYTES),
        name="hgrn_mixer",
    )


def _ffn_kernel(x_ref, mod_ref, ng_ref, wup_ref, cw_ref, cb_ref, wdown_ref,
                fg_ref, o_ref, h_scr, p_scr, halo_scr, *, final_norm):
    tm, d = x_ref.shape
    hidden = wdown_ref.shape[0]

    @pl.when(pl.program_id(1) == 0)
    def _():
        halo_scr[...] = jnp.zeros_like(halo_scr)

    x = x_ref[...]
    mod = mod_ref[0]
    h = _modulated_rms_norm(x, ng_ref[...], mod[4:5], mod[3:4])
    h_scr[...] = h.astype(BF16)

    cn = FFN_BLOCK
    row = lax.broadcasted_iota(jnp.int32, (tm, cn), 0)

    def conv(cols):
        a = jnp.dot(h_scr[...], wup_ref[:, cols], preferred_element_type=F32)
        prev1 = halo_scr[1:2, cols]
        prev2 = halo_scr[0:1, cols]
        halo_scr[:, cols] = a[tm - 2:tm, :]
        a1 = jnp.where(row == 0, prev1, pltpu.roll(a, 1, 0))
        a2 = jnp.where(row == 0, prev2,
                       jnp.where(row == 1, prev1, pltpu.roll(a, 2, 0)))
        return (SQRT_HALF * cb_ref[:, cols]
                + (SQRT_HALF * cw_ref[0:1, cols]) * a2
                + (SQRT_HALF * cw_ref[1:2, cols]) * a1
                + (SQRT_HALF * cw_ref[2:3, cols]) * a)

    for j in range(hidden // cn):
        gate = conv(slice(j * cn, (j + 1) * cn))
        val = conv(slice(hidden + j * cn, hidden + (j + 1) * cn))
        p_scr[:, j * cn:(j + 1) * cn] = (
            gate * (1.0 + lax.erf(gate)) * val).astype(BF16)

    y = jnp.dot(p_scr[...], wdown_ref[...], preferred_element_type=F32)
    out = x + mod[5:6] * y
    if final_norm:
        ms = jnp.mean(out * out, axis=-1, keepdims=True)
        out = (out * lax.rsqrt(ms + EPS)) * fg_ref[...]
    o_ref[...] = out


def _ffn_layer(x2, mod, norm_g, w_up, conv_w, layer, conv_b, w_down, final_g,
               batch, seq, final_norm, next_weights):
    n, d = x2.shape
    tm = TOKEN_TILE
    tiles_per_seq = seq // tm
    hidden = w_down.shape[0]
    return _call_with_casts(
        functools.partial(_ffn_kernel, final_norm=final_norm), 8, next_weights,
        grid=(batch, tiles_per_seq),
        step_of=lambda b, t: b * tiles_per_seq + t,
        in_specs=[
            pl.BlockSpec((tm, d), lambda b, t: (b * tiles_per_seq + t, 0)),
            pl.BlockSpec((1, 6, d), lambda b, t: (b, 0, 0)),
            _resident((1, d)),
            _resident(w_up.shape),
            _resident_layer(conv_w.shape, layer),
            _resident((1, 2 * hidden)),
            _resident(w_down.shape),
            _resident((1, d)),
        ],
        out_spec=pl.BlockSpec((tm, d), lambda b, t: (b * tiles_per_seq + t, 0)),
        out_shape=jax.ShapeDtypeStruct((n, d), F32),
        inputs=(x2, mod, norm_g.reshape(1, d), w_up, conv_w,
                conv_b.reshape(1, 2 * hidden), w_down, final_g.reshape(1, d)),
        scratch_shapes=[
            pltpu.VMEM((tm, d), BF16),
            pltpu.VMEM((tm, hidden), BF16),
            pltpu.VMEM((CONV_WIDTH - 1, 2 * hidden), F32),
        ],
        compiler_params=pltpu.CompilerParams(
            dimension_semantics=("arbitrary", "arbitrary"),
            vmem_limit_bytes=VMEM_LIMIT_BYTES),
        name="conv_ffn",
    )


def kernel(x, c, gm_w_in, gm_ln_g, gm_ln_b, gm_w_s, gm_b_s, gm_w_out, hg_w_in, hg_lb, hg_gn_g, hg_w_out, ffn_w_up, ffn_conv_w, ffn_conv_b, ffn_w_down, norm_g, ada_w, ada_b, final_g):
    batch, seq, d = x.shape
    depth = ada_w.shape[0]
    assert seq % TOKEN_TILE == 0 and TOKEN_TILE % GM_BLOCK == 0

    def mixer_weights(i):
        if i % 2 == 0:
            return ((gm_w_in, i // 2), (gm_w_out, i // 2))
        return ((hg_w_in, i // 2), (hg_w_out, i // 2))

    c_pad = jnp.pad(c, ((0, 8 - batch), (0, 0)))
    mod_all, w_a, w_b = _ada_modulation(c_pad, ada_w, ada_b, mixer_weights(0))
    mod_all = mod_all[:, :batch].reshape(depth, batch, 6, d)

    head_dim = gm_w_out.shape[1] // GM_HEADS
    x2 = x.reshape(batch * seq, d)
    for i in range(depth):
        j = i // 2
        ffn_weights = ((ffn_w_up, i), (ffn_w_down, i))
        if i % 2 == 0:
            b_s_tile = jnp.repeat(gm_b_s[j].T, head_dim, axis=1)
            x2, w_a, w_b = _gmlp_layer(
                x2, mod_all[i], norm_g[i, 0], w_a, gm_ln_g[j], gm_ln_b[j], gm_w_s,
                j, b_s_tile, w_b, seq, ffn_weights)
        else:
            x2, w_a, w_b = _hgrn_layer(
                x2, mod_all[i], norm_g[i, 0], w_a, hg_lb, j, hg_gn_g[j], w_b,
                batch, seq, ffn_weights)
        last = i == depth - 1
        x2, *next_mixer = _ffn_layer(
            x2, mod_all[i], norm_g[i, 1], w_a, ffn_conv_w, i, ffn_conv_b[i], w_b,
            final_g, batch, seq, final_norm=last,
            next_weights=() if last else mixer_weights(i + 1))
        if not last:
            w_a, w_b = next_mixer
    return x2.reshape(batch, seq, d)
```

```python
import functools

import jax
import jax.numpy as jnp
from jax import lax
from jax.experimental import pallas as pl
from jax.experimental.pallas import tpu as pltpu

F32 = jnp.float32
BF16 = jnp.bfloat16

EPS = 1e-6
CHUNK = 64
GM_BLOCK = 128
GM_HEADS = 8
HG_HEADS = 8
HG_DIM = 128
SCAN_CHUNK = 64
SCAN_HALVES = (32, 16, 8, 4, 2)
CHUNKS_PER_GROUP = 1
SCAN_SCHEDULE = ("F0a S1a F1a S2a F2a S3a F3a M0a M1a M2a B0a M3a B1a B2a U0a U1a "
                 "B3a U2a U3a N0")
VREG_ROWS = 8
BF16_TILE_ROWS = 16
LOG2_E = 1.4426950408889634
CONV_WIDTH = 3
FFN_BLOCK = 256
SQRT_HALF = 0.7071067811865476

TOKEN_TILE = 512
VMEM_LIMIT_BYTES = 56 * 1024 * 1024


def _gelu(x):
    return 0.5 * x * (1.0 + lax.erf(x * SQRT_HALF))


def _silu(x):
    return x * (0.5 * jnp.tanh(0.5 * x) + 0.5)


def _modulated_rms_norm(x, gain, scale, shift):
    ms = jnp.mean(x * x, axis=-1, keepdims=True)
    return (x * lax.rsqrt(ms + EPS)) * (gain * (1.0 + scale)) + shift


def _resident(shape):
    zeros = (0,) * len(shape)
    return pl.BlockSpec(shape, lambda *_: zeros, pipeline_mode=pl.Buffered(1))


def _resident_layer(stacked_shape, layer):
    index = (layer,) + (0,) * (len(stacked_shape) - 1)
    return pl.BlockSpec((None,) + tuple(stacked_shape[1:]), lambda *_: index,
                        pipeline_mode=pl.Buffered(1))


def _cast_specs(stacked, layer, grid_steps, step_of):
    _, rows, cols = stacked.shape
    block = next(r for r in range(BF16_TILE_ROWS, rows + 1, BF16_TILE_ROWS)
                 if rows % r == 0 and r * grid_steps >= rows)
    last = rows // block - 1

    def block_index(*grid_indices):
        return jnp.minimum(step_of(*grid_indices), last)

    return (pl.BlockSpec((None, block, cols),
                         lambda *g: (layer, block_index(*g), 0)),
            pl.BlockSpec((block, cols), lambda *g: (block_index(*g), 0)),
            jax.ShapeDtypeStruct((rows, cols), BF16))


def _with_casts(body, n_inputs, n_casts):
    def kernel(*refs):
        out_at = n_inputs + n_casts
        for src, dst in zip(refs[n_inputs:out_at],
                            refs[out_at + 1:out_at + 1 + n_casts]):
            dst[...] = src[...].astype(BF16)
        body(*refs[:n_inputs], refs[out_at], *refs[out_at + 1 + n_casts:])
    return kernel


def _call_with_casts(body, n_inputs, next_weights, grid, step_of, in_specs,
                     out_spec, out_shape, inputs, **kwargs):
    steps = 1
    for extent in grid:
        steps *= extent
    casts = [_cast_specs(w, layer, steps, step_of) for w, layer in next_weights]
    return pl.pallas_call(
        _with_casts(body, n_inputs, len(casts)),
        grid=grid,
        in_specs=list(in_specs) + [c[0] for c in casts],
        out_specs=[out_spec] + [c[1] for c in casts],
        out_shape=[out_shape] + [c[2] for c in casts],
        **kwargs,
    )(*inputs, *[w for w, _ in next_weights])


def _ada_kernel(c_ref, w_ref, b_ref, o_ref):
    cond = _silu(c_ref[...])
    o_ref[0] = jnp.dot(cond, w_ref[0], preferred_element_type=F32) + b_ref[0]


def _ada_modulation(c_pad, ada_w, ada_b, next_weights):
    depth, d, n = ada_w.shape
    rows = c_pad.shape[0]
    tn = 1536
    return _call_with_casts(
        _ada_kernel, 3, next_weights,
        grid=(depth, n // tn),
        step_of=lambda l, j: l * (n // tn) + j,
        in_specs=[
            pl.BlockSpec((rows, d), lambda l, j: (0, 0)),
            pl.BlockSpec((1, d, tn), lambda l, j: (l, 0, j)),
            pl.BlockSpec((1, 1, tn), lambda l, j: (l, 0, j)),
        ],
        out_spec=pl.BlockSpec((1, rows, tn), lambda l, j: (l, 0, j)),
        out_shape=jax.ShapeDtypeStruct((depth, rows, n), F32),
        inputs=(c_pad, ada_w, ada_b.reshape(depth, 1, n)),
        compiler_params=pltpu.CompilerParams(
            dimension_semantics=("arbitrary", "arbitrary"),
            vmem_limit_bytes=VMEM_LIMIT_BYTES),
        name="ada_modulation",
    )


def _gmlp_kernel(x_ref, mod_ref, ng_ref, win_ref, lng_ref, lnb_ref, ws_ref,
                 bs_ref, wout_ref, o_ref, h_scr, u_scr, v_scr, vn_scr, gt_scr):
    tm, d = x_ref.shape
    width = u_scr.shape[1]
    head_dim = width // GM_HEADS
    x = x_ref[...]
    mod = mod_ref[0]
    h = _modulated_rms_norm(x, ng_ref[...], mod[1:2], mod[0:1])
    h_scr[...] = h.astype(BF16)

    cn = 512

    row_sums = []

    def store_gelu(dst_ref, j, z):
        act = _gelu(z)
        dst_ref[:, j * cn:(j + 1) * cn] = act
        if dst_ref is v_scr:
            row_sums.append(jnp.sum(act, axis=-1, keepdims=True))

    def layer_norm_v():
        block_cols = [slice(j * cn, (j + 1) * cn) for j in range(width // cn)]
        mu = functools.reduce(jnp.add, row_sums) * (1.0 / width)
        sq = [jnp.sum(jnp.square(v_scr[:, cols] - mu), axis=-1, keepdims=True)
              for cols in block_cols]
        rstd = lax.rsqrt(functools.reduce(jnp.add, sq) * (1.0 / width) + EPS)
        for cols in block_cols:
            vn = (v_scr[:, cols] - mu) * rstd * lng_ref[:, cols] + lnb_ref[:, cols]
            vn_scr[:, cols] = vn.astype(BF16)

    row = lax.broadcasted_iota(jnp.int32, (GM_BLOCK, GM_BLOCK), 0)
    col = lax.broadcasted_iota(jnp.int32, (GM_BLOCK, GM_BLOCK), 1)
    allowed = (row >= CHUNK) | (col < CHUNK)

    def gate(hd):
        w = jnp.where(allowed, ws_ref[hd], 0.0).astype(BF16)
        cols = slice(hd * head_dim, (hd + 1) * head_dim)
        for r in range(tm // GM_BLOCK):
            rows = slice(r * GM_BLOCK, (r + 1) * GM_BLOCK)
            s = jnp.dot(w, vn_scr[rows, cols], preferred_element_type=F32)
            s = s + bs_ref[:, cols]
            gt_scr[rows, cols] = (u_scr[rows, cols] * s).astype(BF16)

    blocks = width // cn
    heads_per_block = cn // head_dim
    elementwise = []
    for dst_ref, first_col in ((v_scr, width), (u_scr, 0)):
        for j in range(blocks):
            cols = slice(first_col + j * cn, first_col + (j + 1) * cn)
            z = jnp.dot(h_scr[...], win_ref[:, cols], preferred_element_type=F32)
            for work in elementwise:
                work()
            elementwise = [functools.partial(store_gelu, dst_ref, j, z)]
            if dst_ref is v_scr and j == blocks - 1:
                elementwise.append(layer_norm_v)
            if dst_ref is u_scr:
                elementwise += [functools.partial(gate, hd) for hd in range(
                    j * heads_per_block, (j + 1) * heads_per_block)]
    for work in elementwise:
        work()
    y = jnp.dot(gt_scr[...], wout_ref[...], preferred_element_type=F32)
    o_ref[...] = x + mod[2:3] * y


def _gmlp_layer(x2, mod, norm_g, w_in, ln_g, ln_b, w_s, layer, b_s_tile, w_out,
                seq, next_weights):
    n, d = x2.shape
    tm = TOKEN_TILE
    width = w_out.shape[0]
    tiles_per_seq = seq // tm
    return _call_with_casts(
        _gmlp_kernel, 9, next_weights,
        grid=(n // tm,),
        step_of=lambda t: t,
        in_specs=[
            pl.BlockSpec((tm, d), lambda t: (t, 0)),
            pl.BlockSpec((1, 6, d), lambda t: (t // tiles_per_seq, 0, 0)),
            _resident((1, d)),
            _resident(w_in.shape),
            _resident((1, width)),
            _resident((1, width)),
            _resident_layer(w_s.shape, layer),
            _resident(b_s_tile.shape),
            _resident(w_out.shape),
        ],
        out_spec=pl.BlockSpec((tm, d), lambda t: (t, 0)),
        out_shape=jax.ShapeDtypeStruct((n, d), F32),
        inputs=(x2, mod, norm_g.reshape(1, d), w_in, ln_g.reshape(1, width),
                ln_b.reshape(1, width), w_s, b_s_tile, w_out),
        scratch_shapes=[
            pltpu.VMEM((tm, d), BF16),
            pltpu.VMEM((tm, width), F32),
            pltpu.VMEM((tm, width), F32),
            pltpu.VMEM((tm, width), BF16),
            pltpu.VMEM((tm, width), BF16),
        ],
        compiler_params=pltpu.CompilerParams(
            dimension_semantics=("arbitrary",),
            vmem_limit_bytes=VMEM_LIMIT_BYTES),
        name="gmlp_mixer",
    )


def _chunk_cumsum(g):
    groups = SCAN_CHUNK // VREG_ROWS
    g3 = g.reshape(groups, VREG_ROWS, g.shape[1])
    sub = lax.broadcasted_iota(jnp.int32, g3.shape, 1)
    step = 1
    while step < VREG_ROWS:
        g3 = g3 + jnp.where(sub >= step, pltpu.roll(g3, step, 1), 0.0)
        step *= 2
    pieces = [g3[0]]
    for r in range(1, groups):
        pieces.append(g3[r] + pieces[-1][VREG_ROWS - 1:VREG_ROWS, :])
    return jnp.concatenate(pieces, axis=0)


def _split_operands(q, k, cum, g, half):
    lanes = q.shape[1]
    if half >= VREG_ROWS:
        zeros = jnp.zeros((half, lanes), F32)
        q_parts, k_parts = [], []
        for start in range(0, SCAN_CHUNK, 2 * half):
            mid = start + half
            bnd = cum[mid - 1:mid, :]
            k_parts += [k[start:mid] * jnp.exp2(bnd - cum[start:mid]), zeros]
            q_parts += [zeros,
                        q[mid:mid + half] * jnp.exp2(cum[mid:mid + half] - bnd)]
        return jnp.concatenate(q_parts, axis=0), jnp.concatenate(k_parts, axis=0)
    groups = SCAN_CHUNK // VREG_ROWS
    cum3 = cum.reshape(groups, VREG_ROWS, lanes)
    sub = lax.broadcasted_iota(jnp.int32, cum3.shape, 1)
    bnd = None
    for start in range(0, VREG_ROWS, 2 * half):
        piece = jnp.broadcast_to(
            cum3[:, start + half - 1:start + half, :], cum3.shape)
        bnd = piece if bnd is None else jnp.where(sub >= start, piece, bnd)
    e = jnp.exp2(-jnp.abs(cum3 - bnd)).reshape(SCAN_CHUNK, lanes)
    return q * e, k * e


def _stage_head(rows, lanes, q_ref, k_ref, v_ref, g_ref, chunk, stage):
    ql_ref, ku_ref, qin_ref, kout_ref, v16_ref, direct_ref, decay_ref = stage
    q, k, v, g = (ref[rows, lanes] for ref in (q_ref, k_ref, v_ref, g_ref))
    cum = _chunk_cumsum(g)
    last = cum[SCAN_CHUNK - 1:SCAN_CHUNK, :]
    qin_ref[rows, lanes] = (q * jnp.exp2(cum)).astype(BF16)
    kout_ref[rows, lanes] = (k * jnp.exp2(last - cum)).astype(BF16)
    v16_ref[rows, lanes] = v.astype(BF16)
    decay_ref[chunk, :, lanes] = jnp.exp2(last)
    groups = SCAN_CHUNK // VREG_ROWS

    def previous_row(a):
        a3 = a.reshape(groups, VREG_ROWS, a.shape[1])
        return pltpu.roll(a3, 1, 1).reshape(a.shape)

    odd = (lax.broadcasted_iota(jnp.int32, q.shape, 0) & 1) == 1
    diagonal = jnp.sum(q * k, axis=-1, keepdims=True)
    adjacent = jnp.sum(jnp.where(odd, q * previous_row(k) * jnp.exp2(g), 0.0),
                       axis=-1, keepdims=True)
    direct_ref[rows, lanes] = diagonal * v + adjacent * previous_row(v)
    for idx, half in enumerate(SCAN_HALVES):
        q_l, k_u = _split_operands(q, k, cum, g, half)
        ql_ref[idx, rows, lanes] = q_l.astype(BF16)
        ku_ref[idx, rows, lanes] = k_u.astype(BF16)


def _block_diag(a, b):
    zeros = jnp.zeros_like(a)
    return jnp.concatenate([jnp.concatenate([a, zeros], axis=1),
                            jnp.concatenate([zeros, b], axis=1)], axis=0)


def _by_key_position(a, b):
    half = SCAN_CHUNK // 2
    return jnp.concatenate([_block_diag(a[:half], b[:half]),
                            _block_diag(a[half:], b[half:])], axis=0)


def _combine_head_pair(pair, rows, chunk, level_of_pair, stage, state_ref):
    ql_ref, ku_ref, qin_ref, kout_ref, v16_ref, direct_ref, decay_ref = stage
    nt = (((1,), (1,)), ((), ()))
    first, second = 2 * pair, 2 * pair + 1
    lanes_a = slice(first * HG_DIM, second * HG_DIM)
    lanes_b = slice(second * HG_DIM, (second + 1) * HG_DIM)
    lanes = slice(first * HG_DIM, (second + 1) * HG_DIM)

    def level_scores(idx):
        keys = _by_key_position(ku_ref[idx, rows, lanes_a],
                                ku_ref[idx, rows, lanes_b])
        if idx == 0:
            keys = keys[:SCAN_CHUNK]
        s = lax.dot_general(ql_ref[idx, rows, lanes], keys, nt,
                            preferred_element_type=F32)
        return jnp.concatenate([s, jnp.zeros_like(s)], axis=1) if idx == 0 else s

    level_scores = [level_scores(idx) for idx in range(len(SCAN_HALVES))]
    state_a, state_b = state_ref[first], state_ref[second]
    out = direct_ref[rows, lanes] + lax.dot_general(
        qin_ref[rows, lanes],
        _block_diag(state_a.astype(BF16), state_b.astype(BF16)), nt,
        preferred_element_type=F32)
    v_bd = _by_key_position(v16_ref[rows, lanes_a], v16_ref[rows, lanes_b])

    def advance_state():
        update = lax.dot_general(
            v_bd, _by_key_position(kout_ref[rows, lanes_a], kout_ref[rows, lanes_b]),
            (((0,), (0,)), ((), ())), preferred_element_type=F32)
        state_ref[first] = (state_a * decay_ref[chunk, :, lanes_a]
                            + update[:HG_DIM, :HG_DIM])
        state_ref[second] = (state_b * decay_ref[chunk, :, lanes_b]
                             + update[HG_DIM:, HG_DIM:])

    def finish():
        scores = level_scores[0]
        for half, s in zip(SCAN_HALVES[1:], level_scores[1:]):
            scores = jnp.where(level_of_pair[half], s, scores)
        weighted = jnp.dot(scores.astype(BF16), v_bd, preferred_element_type=F32)
        return lambda: out + weighted

    return finish, advance_state


def _hgrn_lower_bound(lb_raw, layer):
    e = jnp.exp(lb_raw - jnp.max(lb_raw, axis=0, keepdims=True))
    total = jnp.sum(e, axis=0, keepdims=True)
    below = jnp.zeros_like(total)
    for m in range(1, layer + 1):
        below = below + e[m:m + 1]
    return below / total


def _hgrn_kernel(x_ref, mod_ref, ng_ref, win_ref, lb_ref, gn_ref, wout_ref,
                 o_ref, h_scr, q_scr, k_scr, v_scr, lf_scr, sg_scr, o_scr,
                 state_scr, ql_scr, ku_scr, qin_scr, kout_scr, v16_scr,
                 decay_scr, *, layer):
    tm, d = x_ref.shape
    stage = (ql_scr, ku_scr, qin_scr, kout_scr, v16_scr, o_scr, decay_scr)

    @pl.when(pl.program_id(1) == 0)
    def _():
        state_scr[...] = jnp.zeros_like(state_scr)

    x = x_ref[...]
    mod = mod_ref[0]
    h = _modulated_rms_norm(x, ng_ref[...], mod[1:2], mod[0:1])
    h_scr[...] = h.astype(BF16)

    def proj(i):
        return jnp.dot(h_scr[...], win_ref[:, i * d:(i + 1) * d],
                       preferred_element_type=F32)

    lb = _hgrn_lower_bound(lb_ref[...], layer)
    q_lin = proj(0)
    f_lin = proj(1)
    q_scr[...] = _silu(q_lin)
    v_scr[...] = proj(2)
    f = lb + (1.0 - lb) * jax.nn.sigmoid(f_lin)
    lf_scr[...] = jnp.log(f) * LOG2_E
    k_scr[...] = 1.0 - f
    sg_scr[...] = _silu(proj(3))

    head_lanes = [slice(hd * HG_DIM, (hd + 1) * HG_DIM)
                  for hd in range(HG_HEADS)]

    def chunk_rows(c):
        start = c * SCAN_CHUNK
        if not isinstance(c, int):
            start = pl.multiple_of(start, SCAN_CHUNK)
        return pl.ds(start, SCAN_CHUNK)

    def normalize_pair(rows, pair, out):
        for hd in (2 * pair, 2 * pair + 1):
            o = out[:, (hd % 2) * HG_DIM:(hd % 2 + 1) * HG_DIM]
            ms = jnp.mean(o * o, axis=-1, keepdims=True)
            o_scr[rows, head_lanes[hd]] = o * lax.rsqrt(ms + EPS)

    def stage_pair(c, pair):
        for hd in (2 * pair, 2 * pair + 1):
            _stage_head(chunk_rows(c), head_lanes[hd], q_scr, k_scr, v_scr,
                        lf_scr, c, stage)

    def chunk_group_body(group, carry, schedule):
        first_chunk = group * CHUNKS_PER_GROUP
        half_chunk = SCAN_CHUNK // 2
        row_s = lax.broadcasted_iota(jnp.int32, (SCAN_CHUNK, 2 * SCAN_CHUNK), 0)
        lane_s = lax.broadcasted_iota(jnp.int32, (SCAN_CHUNK, 2 * SCAN_CHUNK), 1)
        col_s = ((lane_s >> (SCAN_CHUNK.bit_length() - 1)) * half_chunk
                 + (lane_s & (half_chunk - 1)))
        level_of_pair = {
            half: (((row_s ^ col_s) >> (half.bit_length() - 1)) == 1)
            & (row_s > col_s) for half in SCAN_HALVES[1:]}
        finish, advance_state, emit = {}, {}, {}
        for step in schedule.split():
            kind, pair = step[0], int(step[1])
            if kind == "N":
                stage_pair(first_chunk + CHUNKS_PER_GROUP, pair)
                continue
            c = first_chunk + "ab".index(step[2])
            item = step[1:]
            if kind == "S":
                stage_pair(c, pair)
            elif kind == "F":
                finish[item], advance_state[item] = _combine_head_pair(
                    pair, chunk_rows(c), c, level_of_pair, stage, state_scr)
            elif kind == "M":
                emit[item] = finish[item]()
            elif kind == "B":
                normalize_pair(chunk_rows(c), pair, emit[item]())
            else:
                advance_state[item]()
        return carry

    groups = tm // (SCAN_CHUNK * CHUNKS_PER_GROUP)
    stage_pair(0, 0)
    lax.fori_loop(0, groups - 1,
                  functools.partial(chunk_group_body, schedule=SCAN_SCHEDULE), 0)
    chunk_group_body(groups - 1, 0, SCAN_SCHEDULE.replace(" N0", ""))

    gated = (o_scr[...] * gn_ref[...]) * sg_scr[...]
    y = jnp.dot(gated.astype(BF16), wout_ref[...], preferred_element_type=F32)
    o_ref[...] = x + mod[2:3] * y


def _hgrn_layer(x2, mod, norm_g, w_in, lb_raw, layer, gn_g, w_out, batch, seq,
                next_weights):
    n, d = x2.shape
    tm = TOKEN_TILE
    tiles_per_seq = seq // tm
    act = pltpu.VMEM((tm, d), F32)
    return _call_with_casts(
        functools.partial(_hgrn_kernel, layer=layer), 7, next_weights,
        grid=(batch, tiles_per_seq),
        step_of=lambda b, t: b * tiles_per_seq + t,
        in_specs=[
            pl.BlockSpec((tm, d), lambda b, t: (b * tiles_per_seq + t, 0)),
            pl.BlockSpec((1, 6, d), lambda b, t: (b, 0, 0)),
            _resident((1, d)),
            _resident(w_in.shape),
            _resident(lb_raw.shape),
            _resident((1, d)),
            _resident(w_out.shape),
        ],
        out_spec=pl.BlockSpec((tm, d), lambda b, t: (b * tiles_per_seq + t, 0)),
        out_shape=jax.ShapeDtypeStruct((n, d), F32),
        inputs=(x2, mod, norm_g.reshape(1, d), w_in, lb_raw.astype(F32),
                gn_g.reshape(1, d), w_out),
        scratch_shapes=[
            pltpu.VMEM((tm, d), BF16),
            act, act, act, act, act, act,
            pltpu.VMEM((HG_HEADS, HG_DIM, HG_DIM), F32),
            pltpu.VMEM((len(SCAN_HALVES), tm, d), BF16),
            pltpu.VMEM((len(SCAN_HALVES), tm, d), BF16),
            pltpu.VMEM((tm, d), BF16),
            pltpu.VMEM((tm, d), BF16),
            pltpu.VMEM((tm, d), BF16),
            pltpu.VMEM((tm // SCAN_CHUNK, 1, d), F32),
        ],
        compiler_params=pltpu.CompilerParams(
            dimension_semantics=("arbitrary", "arbitrary"),
            vmem_limit_bytes=VMEM_LIMIT_BYTES),
        name="hgrn_mixer",
    )


def _ffn_kernel(x_ref, mod_ref, ng_ref, wup_ref, cw_ref, cb_ref, wdown_ref,
                fg_ref, o_ref, h_scr, p_scr, halo_scr, *, final_norm):
    tm, d = x_ref.shape
    hidden = wdown_ref.shape[0]

    @pl.when(pl.program_id(1) == 0)
    def _():
        halo_scr[...] = jnp.zeros_like(halo_scr)

    x = x_ref[...]
    mod = mod_ref[0]
    h = _modulated_rms_norm(x, ng_ref[...], mod[4:5], mod[3:4])
    h_scr[...] = h.astype(BF16)

    cn = FFN_BLOCK
    row = lax.broadcasted_iota(jnp.int32, (tm, cn), 0)

    def conv(cols):
        a = jnp.dot(h_scr[...], wup_ref[:, cols], preferred_element_type=F32)
        prev1 = halo_scr[1:2, cols]
        prev2 = halo_scr[0:1, cols]
        halo_scr[:, cols] = a[tm - 2:tm, :]
        a1 = jnp.where(row == 0, prev1, pltpu.roll(a, 1, 0))
        a2 = jnp.where(row == 0, prev2,
                       jnp.where(row == 1, prev1, pltpu.roll(a, 2, 0)))
        return (SQRT_HALF * cb_ref[:, cols]
                + (SQRT_HALF * cw_ref[0:1, cols]) * a2
                + (SQRT_HALF * cw_ref[1:2, cols]) * a1
                + (SQRT_HALF * cw_ref[2:3, cols]) * a)

    for j in range(hidden // cn):
        gate = conv(slice(j * cn, (j + 1) * cn))
        val = conv(slice(hidden + j * cn, hidden + (j + 1) * cn))
        p_scr[:, j * cn:(j + 1) * cn] = (
            gate * (1.0 + lax.erf(gate)) * val).astype(BF16)

    y = jnp.dot(p_scr[...], wdown_ref[...], preferred_element_type=F32)
    out = x + mod[5:6] * y
    if final_norm:
        ms = jnp.mean(out * out, axis=-1, keepdims=True)
        out = (out * lax.rsqrt(ms + EPS)) * fg_ref[...]
    o_ref[...] = out


def _ffn_layer(x2, mod, norm_g, w_up, conv_w, layer, conv_b, w_down, final_g,
               batch, seq, final_norm, next_weights):
    n, d = x2.shape
    tm = TOKEN_TILE
    tiles_per_seq = seq // tm
    hidden = w_down.shape[0]
    return _call_with_casts(
        functools.partial(_ffn_kernel, final_norm=final_norm), 8, next_weights,
        grid=(batch, tiles_per_seq),
        step_of=lambda b, t: b * tiles_per_seq + t,
        in_specs=[
            pl.BlockSpec((tm, d), lambda b, t: (b * tiles_per_seq + t, 0)),
            pl.BlockSpec((1, 6, d), lambda b, t: (b, 0, 0)),
            _resident((1, d)),
            _resident(w_up.shape),
            _resident_layer(conv_w.shape, layer),
            _resident((1, 2 * hidden)),
            _resident(w_down.shape),
            _resident((1, d)),
        ],
        out_spec=pl.BlockSpec((tm, d), lambda b, t: (b * tiles_per_seq + t, 0)),
        out_shape=jax.ShapeDtypeStruct((n, d), F32),
        inputs=(x2, mod, norm_g.reshape(1, d), w_up, conv_w,
                conv_b.reshape(1, 2 * hidden), w_down, final_g.reshape(1, d)),
        scratch_shapes=[
            pltpu.VMEM((tm, d), BF16),
            pltpu.VMEM((tm, hidden), BF16),
            pltpu.VMEM((CONV_WIDTH - 1, 2 * hidden), F32),
        ],
        compiler_params=pltpu.CompilerParams(
            dimension_semantics=("arbitrary", "arbitrary"),
            vmem_limit_bytes=VMEM_LIMIT_BYTES),
        name="conv_ffn",
    )


def kernel(x, c, gm_w_in, gm_ln_g, gm_ln_b, gm_w_s, gm_b_s, gm_w_out, hg_w_in, hg_lb, hg_gn_g, hg_w_out, ffn_w_up, ffn_conv_w, ffn_conv_b, ffn_w_down, norm_g, ada_w, ada_b, final_g):
    batch, seq, d = x.shape
    depth = ada_w.shape[0]
    assert seq % TOKEN_TILE == 0 and TOKEN_TILE % GM_BLOCK == 0

    def mixer_weights(i):
        if i % 2 == 0:
            return ((gm_w_in, i // 2), (gm_w_out, i // 2))
        return ((hg_w_in, i // 2), (hg_w_out, i // 2))

    c_pad = jnp.pad(c, ((0, 8 - batch), (0, 0)))
    mod_all, w_a, w_b = _ada_modulation(c_pad, ada_w, ada_b, mixer_weights(0))
    mod_all = mod_all[:, :batch].reshape(depth, batch, 6, d)

    head_dim = gm_w_out.shape[1] // GM_HEADS
    x2 = x.reshape(batch * seq, d)
    for i in range(depth):
        j = i // 2
        ffn_weights = ((ffn_w_up, i), (ffn_w_down, i))
        if i % 2 == 0:
            b_s_tile = jnp.repeat(gm_b_s[j].T, head_dim, axis=1)
            x2, w_a, w_b = _gmlp_layer(
                x2, mod_all[i], norm_g[i, 0], w_a, gm_ln_g[j], gm_ln_b[j], gm_w_s,
                j, b_s_tile, w_b, seq, ffn_weights)
        else:
            x2, w_a, w_b = _hgrn_layer(
                x2, mod_all[i], norm_g[i, 0], w_a, hg_lb, j, hg_gn_g[j], w_b,
                batch, seq, ffn_weights)
        last = i == depth - 1
        x2, *next_mixer = _ffn_layer(
            x2, mod_all[i], norm_g[i, 1], w_a, ffn_conv_w, i, ffn_conv_b[i], w_b,
            final_g, batch, seq, final_norm=last,
            next_weights=() if last else mixer_weights(i + 1))
        if not last:
            w_a, w_b = next_mixer
    return x2.reshape(batch, seq, d)
```
